```python
import jax, jax.numpy as jnp
from jax import lax
import numpy as np

D_MODEL = 2048
BATCH = 8
SEQ = 2048
DEPTH = 2
DEC_BATCH = 128
DEC_SEQ = 8
PAST_LEN = 2048
PAGE_SIZE = 128

N_MIXERS = 2
N_SB_LAYERS = (DEPTH + 1) // 2
N_RET_LAYERS = DEPTH // 2
SB_HEADS = 16
SB_HEAD_DIM = D_MODEL // SB_HEADS
SB_BIAS_HI = -5.0
SB_BIAS_LO = -9.0
Q_BLOCK = 128
RET_HEADS = 8
RET_QK_DIM = D_MODEL // RET_HEADS
RET_V_DIM = 2 * D_MODEL // RET_HEADS
RET_CHUNK = 128
ROPE_BASE = 10000.0
PEER_HEADS = 8
PEER_KEY_DIM = 128
N_KEYS = 128
N_EXPERTS = N_KEYS * N_KEYS
PEER_TOPK = 16
PEER_TOKEN_BLOCK = 128
NORM_EPS = 1e-6

kernel_name = 'hybrid_stickbreak_retention_peer_step'


def _rmsnorm(x, gain):
    xf = x.astype(jnp.float32)
    y = xf * lax.rsqrt(jnp.mean(xf * xf, axis=-1, keepdims=True) + NORM_EPS)
    return (y * gain.astype(jnp.float32)).astype(x.dtype)


def _sb_weights(z, mask):
    log_beta = jax.nn.log_sigmoid(z)
    log_rest = jnp.where(mask, jax.nn.log_sigmoid(-z), 0.0)
    between = lax.cumsum(log_rest, axis=z.ndim - 1, reverse=True) - log_rest
    return jnp.where(mask, jnp.exp(log_beta + between), 0.0)


def _sb_project(x, wqkv, q_gain, k_gain):
    b, s, _ = x.shape
    qkv = (x @ wqkv).reshape(b, s, 3, SB_HEADS, SB_HEAD_DIM)
    q = _rmsnorm(qkv[:, :, 0], q_gain)
    k = _rmsnorm(qkv[:, :, 1], k_gain)
    v = qkv[:, :, 2]
    return q, k, v


def _sb_prompt(q, k, v, bias):
    b, s, h, dh = q.shape
    qb = Q_BLOCK if s % Q_BLOCK == 0 else s
    nb = s // qb
    scale = dh ** -0.5
    pos = jnp.arange(s)
    bias_f = bias.astype(jnp.float32)[None, :, None, None]
    q_blocks = jnp.moveaxis(q.reshape(b, nb, qb, h, dh), 1, 0)
    qpos_blocks = pos.reshape(nb, qb)

    def block(args):
        qblk, qpos = args
        z = jnp.einsum('bqhd,bkhd->bhqk', qblk, k).astype(jnp.float32) * scale + bias_f
        mask = pos[None, :] < qpos[:, None]
        a = _sb_weights(z, mask)
        return jnp.einsum('bhqk,bkhd->bqhd', a.astype(v.dtype), v)

    out = lax.map(block, (q_blocks, qpos_blocks))
    return jnp.moveaxis(out, 0, 1).reshape(b, s, h, dh)


def _sb_sample(q, k_new, v_new, k_past, v_past, bias):
    t = q.shape[1]
    p = k_past.shape[1]
    scale = q.shape[-1] ** -0.5
    bias_f = bias.astype(jnp.float32)[None, :, None, None]
    z = jnp.concatenate([jnp.einsum('bqhd,bkhd->bhqk', q, k_past),
                         jnp.einsum('bqhd,bkhd->bhqk', q, k_new)], axis=-1).astype(jnp.float32) * scale + bias_f
    tpos = jnp.arange(t)
    mask = jnp.concatenate([jnp.ones((t, p), dtype=bool), tpos[None, :] < tpos[:, None]], axis=1)
    a = _sb_weights(z, mask).astype(v_new.dtype)
    return (jnp.einsum('bhqk,bkhd->bqhd', a[..., :p], v_past)
            + jnp.einsum('bhqk,bkhd->bqhd', a[..., p:], v_new))


def _rope(x, pos):
    half = x.shape[-1] // 2
    inv = ROPE_BASE ** (-jnp.arange(half, dtype=jnp.float32) / half)
    ang = pos.astype(jnp.float32)[:, None] * inv[None, :]
    cos = jnp.cos(ang)[None, :, None, :]
    sin = jnp.sin(ang)[None, :, None, :]
    x1 = x[..., :half].astype(jnp.float32)
    x2 = x[..., half:].astype(jnp.float32)
    return jnp.concatenate([x1 * cos - x2 * sin, x1 * sin + x2 * cos], axis=-1).astype(x.dtype)


def _retention_chunks(q, k, v, state):
    b, s, h, dk = q.shape
    dv = v.shape[-1]
    c = RET_CHUNK if s % RET_CHUNK == 0 else s
    nc = s // c
    log_g = jnp.log1p(-jnp.exp2(-5.0 - jnp.arange(h, dtype=jnp.float32)))
    idx = jnp.arange(c, dtype=jnp.float32)
    diff = idx[:, None] - idx[None, :]
    decay_intra = jnp.where(diff[None] >= 0,
                            jnp.exp(jnp.maximum(diff, 0.0)[None] * log_g[:, None, None]), 0.0)
    decay_query = jnp.exp((idx + 1.0)[:, None] * log_g[None, :])
    decay_key = jnp.exp((c - 1.0 - idx)[:, None] * log_g[None, :])
    decay_chunk = jnp.exp(c * log_g)

    def split(a):
        return jnp.moveaxis(a.reshape(b, nc, c, h, a.shape[-1]), 1, 0)

    def step(r, xs):
        qc, kc, vc = xs
        qf = qc.astype(jnp.float32)
        kf = kc.astype(jnp.float32)
        vf = vc.astype(jnp.float32)
        att = jnp.einsum('bihd,bjhd->bhij', qf, kf) * decay_intra
        inner = jnp.einsum('bhij,bjhe->bihe', att, vf)
        cross = jnp.einsum('bihd,bhde->bihe', qf, r) * decay_query[None, :, :, None]
        r_new = (decay_chunk[None, :, None, None] * r
                 + jnp.einsum('bjhd,bjhe->bhde', kf * decay_key[None, :, :, None], vf))
        return r_new, inner + cross

    r_final, out = lax.scan(step, state.astype(jnp.float32), (split(q), split(k), split(v)))
    out = jnp.moveaxis(out, 0, 1).reshape(b, s, h, dv)
    return out, r_final


def _retention_layer(x, pos, state, wqkvg, norm_gain, wo):
    b, s, _ = x.shape
    hq = RET_HEADS * RET_QK_DIM
    hv = RET_HEADS * RET_V_DIM
    proj = x @ wqkvg
    q, k, v, g = jnp.split(proj, [hq, 2 * hq, 2 * hq + hv], axis=-1)
    q = _rope(q.reshape(b, s, RET_HEADS, RET_QK_DIM), pos)
    k = _rope(k.reshape(b, s, RET_HEADS, RET_QK_DIM), pos) * (RET_QK_DIM ** -0.5)
    v = v.reshape(b, s, RET_HEADS, RET_V_DIM)
    o, new_state = _retention_chunks(q, k, v, state)
    o = _rmsnorm(o, norm_gain.reshape(RET_HEADS, RET_V_DIM)).astype(x.dtype).reshape(b, s, hv)
    y = (jax.nn.silu(g) * o) @ wo
    return y, new_state


def _peer(x, wq, keys1, keys2, u, v):
    n, d = x.shape
    tb = PEER_TOKEN_BLOCK
    n_pad = (-n) % tb
    xp = jnp.pad(x, ((0, n_pad), (0, 0))).reshape(-1, tb, d)
    half = PEER_KEY_DIM // 2
    kk = PEER_TOPK * PEER_TOPK

    def block(xb):
        q = (xb @ wq).reshape(tb, PEER_HEADS, 2, half)
        s1 = jnp.einsum('thc,nc->thn', q[:, :, 0], keys1).astype(jnp.float32)
        s2 = jnp.einsum('thc,nc->thn', q[:, :, 1], keys2).astype(jnp.float32)
        v1, i1 = lax.top_k(s1, PEER_TOPK)
        v2, i2 = lax.top_k(s2, PEER_TOPK)
        cand = (v1[..., :, None] + v2[..., None, :]).reshape(tb, PEER_HEADS, kk)
        cand_idx = (i1[..., :, None] * N_KEYS + i2[..., None, :]).reshape(tb, PEER_HEADS, kk)
        sc, sel = lax.top_k(cand, PEER_TOPK)
        idx = jnp.take_along_axis(cand_idx, sel, axis=-1)
        gate = jax.nn.softmax(sc, axis=-1)
        act = jax.nn.gelu(jnp.einsum('td,thkd->thk', xb, u[idx]).astype(jnp.float32))
        coef = (gate * act).astype(xb.dtype)
        return jnp.einsum('thk,thkd->td', coef, v[idx])

    out = lax.map(block, xp).reshape(-1, d)
    return out[:n]


def setup_inputs(seed: int = 0) -> dict:
    key = jax.random.key(seed)
    ks = jax.random.split(key, 24)
    f32 = jnp.float32
    n_pages = PAST_LEN // PAGE_SIZE
    used = DEC_BATCH * n_pages
    n_pool = used + (used + 3) // 4

    def nrm(k, shape, scale):
        return jax.random.normal(k, shape, f32) * scale

    hq = RET_HEADS * RET_QK_DIM
    hv = RET_HEADS * RET_V_DIM
    sb_bias_base = jnp.linspace(SB_BIAS_HI, SB_BIAS_LO, SB_HEADS, dtype=f32)
    return {
        'x_prompt': nrm(ks[0], (BATCH, SEQ, D_MODEL), 1.0),
        'x_sample': nrm(ks[1], (DEC_BATCH, DEC_SEQ, D_MODEL), 1.0),
        'cache_k': nrm(ks[2], (N_SB_LAYERS, n_pool, PAGE_SIZE, SB_HEADS, SB_HEAD_DIM), 1.0),
        'cache_v': nrm(ks[3], (N_SB_LAYERS, n_pool, PAGE_SIZE, SB_HEADS, SB_HEAD_DIM), 1.0),
        'state_ret': nrm(ks[4], (N_RET_LAYERS, DEC_BATCH, RET_HEADS, RET_QK_DIM, RET_V_DIM), 0.5),
        'page_table': jax.random.permutation(ks[5], n_pool)[:used].reshape(DEC_BATCH, n_pages).astype(jnp.int32),
        'norm_mix': 1.0 + nrm(ks[6], (DEPTH, D_MODEL), 0.01),
        'norm_ffn': 1.0 + nrm(ks[7], (DEPTH, D_MODEL), 0.01),
        'sb_wqkv': nrm(ks[8], (N_SB_LAYERS, D_MODEL, 3 * D_MODEL), D_MODEL ** -0.5),
        'sb_q_gain': 1.0 + nrm(ks[9], (N_SB_LAYERS, SB_HEAD_DIM), 0.01),
        'sb_k_gain': 1.0 + nrm(ks[10], (N_SB_LAYERS, SB_HEAD_DIM), 0.01),
        'sb_bias': sb_bias_base[None, :] + nrm(ks[20], (N_SB_LAYERS, SB_HEADS), 0.1),
        'sb_wo': nrm(ks[11], (N_SB_LAYERS, D_MODEL, D_MODEL), D_MODEL ** -0.5),
        'ret_wqkvg': nrm(ks[12], (N_RET_LAYERS, D_MODEL, 2 * hq + 2 * hv), D_MODEL ** -0.5),
        'ret_norm_gain': 1.0 + nrm(ks[13], (N_RET_LAYERS, hv), 0.01),
        'ret_wo': nrm(ks[14], (N_RET_LAYERS, hv, D_MODEL), hv ** -0.5),
        'peer_wq': nrm(ks[15], (DEPTH, D_MODEL, PEER_HEADS * PEER_KEY_DIM), D_MODEL ** -0.5),
        'peer_keys1': nrm(ks[16], (DEPTH, N_KEYS, PEER_KEY_DIM // 2), (PEER_KEY_DIM // 2) ** -0.5),
        'peer_keys2': nrm(ks[17], (DEPTH, N_KEYS, PEER_KEY_DIM // 2), (PEER_KEY_DIM // 2) ** -0.5),
        'peer_u': nrm(ks[18], (DEPTH, N_EXPERTS, D_MODEL), D_MODEL ** -0.5),
        'peer_v': nrm(ks[19], (DEPTH, N_EXPERTS, D_MODEL), (PEER_HEADS * PEER_TOPK) ** -0.5),
    }


def reference(x_prompt, x_sample, cache_k, cache_v, state_ret, page_table,
              norm_mix, norm_ffn, sb_wqkv, sb_q_gain, sb_k_gain, sb_bias, sb_wo,
              ret_wqkvg, ret_norm_gain, ret_wo,
              peer_wq, peer_keys1, peer_keys2, peer_u, peer_v):
    bp, sp, d = x_prompt.shape
    bd, sd, _ = x_sample.shape
    n_pages = page_table.shape[1]
    past = n_pages * PAGE_SIZE
    pos_prompt = jnp.arange(sp)
    pos_sample = past + jnp.arange(sd)
    hp = x_prompt
    hs = x_sample
    k_prompt, v_prompt, k_sample, v_sample = [], [], [], []
    ret_prompt, ret_sample = [], []
    for i in range(DEPTH):
        j = i // N_MIXERS
        xp = _rmsnorm(hp, norm_mix[i])
        xs = _rmsnorm(hs, norm_mix[i])
        if i % N_MIXERS == 0:
            qp, kp, vp = _sb_project(xp, sb_wqkv[j], sb_q_gain[j], sb_k_gain[j])
            hp = hp + _sb_prompt(qp, kp, vp, sb_bias[j]).reshape(bp, sp, d) @ sb_wo[j]
            qs, ks_, vs = _sb_project(xs, sb_wqkv[j], sb_q_gain[j], sb_k_gain[j])
            k_past = cache_k[j][page_table].reshape(bd, past, SB_HEADS, SB_HEAD_DIM)
            v_past = cache_v[j][page_table].reshape(bd, past, SB_HEADS, SB_HEAD_DIM)
            hs = hs + _sb_sample(qs, ks_, vs, k_past, v_past, sb_bias[j]).reshape(bd, sd, d) @ sb_wo[j]
            k_prompt.append(kp)
            v_prompt.append(vp)
            k_sample.append(ks_)
            v_sample.append(vs)
        else:
            zero_state = jnp.zeros((bp, RET_HEADS, RET_QK_DIM, RET_V_DIM), jnp.float32)
            yp, rp = _retention_layer(xp, pos_prompt, zero_state, ret_wqkvg[j], ret_norm_gain[j], ret_wo[j])
            ys, rs = _retention_layer(xs, pos_sample, state_ret[j], ret_wqkvg[j], ret_norm_gain[j], ret_wo[j])
            hp = hp + yp
            hs = hs + ys
            ret_prompt.append(rp)
            ret_sample.append(rs)
        hp = hp + _peer(_rmsnorm(hp, norm_ffn[i]).reshape(-1, d), peer_wq[i], peer_keys1[i],
                        peer_keys2[i], peer_u[i], peer_v[i]).reshape(bp, sp, d)
        hs = hs + _peer(_rmsnorm(hs, norm_ffn[i]).reshape(-1, d), peer_wq[i], peer_keys1[i],
                        peer_keys2[i], peer_u[i], peer_v[i]).reshape(bd, sd, d)
    return (hp, hs, jnp.stack(k_prompt), jnp.stack(v_prompt), jnp.stack(k_sample), jnp.stack(v_sample),
            jnp.stack(ret_prompt), jnp.stack(ret_sample))
```

```python
import functools

import jax
import jax.numpy as jnp
from jax import lax
from jax.experimental import pallas as pl
from jax.experimental.pallas import tpu as pltpu

F32 = jnp.float32
BF16 = jnp.bfloat16

SB_HEADS = 16
RET_HEADS = 8
PEER_HEADS = 8
PEER_KEY_DIM = 128
N_KEYS = 128
PEER_TOPK = 16
PAGE_SIZE = 128
Q_BLOCK = 128
RET_CHUNK = 128
ROPE_BASE = 10000.0
NORM_EPS = 1e-6

LANES = 128
SUBLANES = 8
BF16_ROWS = 16
VMEM_LIMIT = 52 * 1024 * 1024

PEER_TOKENS_PER_STEP = 8


def _cparams(sem, vmem=VMEM_LIMIT):
    return pltpu.CompilerParams(dimension_semantics=sem, vmem_limit_bytes=vmem)


def _mm_kernel(*refs, has_gain, has_res, n_extra, epilogue, row_chunk):
    it = iter(refs)
    x_ref = next(it)
    g_ref = next(it) if has_gain else None
    w_ref = next(it)
    r_ref = next(it) if has_res else None
    extra_refs = [next(it) for _ in range(n_extra)]
    o_ref = next(it)
    xn_ref = next(it)
    tm = x_ref.shape[0]

    @pl.when(pl.program_id(1) == 0)
    def _():
        def chunk(c, carry):
            rows = pl.ds(pl.multiple_of(c * row_chunk, row_chunk), row_chunk)
            xv = x_ref[rows, :].astype(F32)
            if has_gain:
                ms = jnp.mean(xv * xv, axis=-1, keepdims=True)
                xv = (xv * lax.rsqrt(ms + NORM_EPS)) * g_ref[...]
            xn_ref[rows, :] = xv.astype(BF16)
            return carry
        lax.fori_loop(0, tm // row_chunk, chunk, 0)

    acc = jnp.dot(xn_ref[...], w_ref[...], preferred_element_type=F32)
    if has_res:
        acc = acc + r_ref[...]
    if epilogue is None:
        o_ref[...] = acc.astype(o_ref.dtype)
    else:
        epilogue(acc, o_ref, extra_refs, pl.program_id(1))


def _norm_matmul(x, w, *, gain=None, res=None, epilogue=None, extras=(), extra_specs=(),
                 tm=512, tn=512, name="mm"):
    n, k = x.shape
    m = w.shape[1]
    tn = min(tn, m)
    assert n % tm == 0 and m % tn == 0, (n, tm, m, tn)
    extra_specs = [make(tm, tn) for make in extra_specs]
    in_specs = [pl.BlockSpec((tm, k), lambda i, j: (i, 0))]
    args = [x]
    if gain is not None:
        in_specs.append(pl.BlockSpec((1, k), lambda i, j: (0, 0)))
        args.append(gain.reshape(1, k).astype(F32))
    in_specs.append(pl.BlockSpec((k, tn), lambda i, j: (0, j)))
    args.append(w)
    if res is not None:
        in_specs.append(pl.BlockSpec((tm, tn), lambda i, j: (i, j)))
        args.append(res)
    in_specs.extend(extra_specs)
    args.extend(extras)
    kern = functools.partial(_mm_kernel, has_gain=gain is not None, has_res=res is not None,
                             n_extra=len(extras), epilogue=epilogue, row_chunk=min(64, tm))
    return pl.pallas_call(
        kern,
        grid=(n // tm, m // tn),
        in_specs=in_specs,
        out_specs=pl.BlockSpec((tm, tn), lambda i, j: (i, j)),
        out_shape=jax.ShapeDtypeStruct((n, m), F32),
        scratch_shapes=[pltpu.VMEM((tm, k), BF16)],
        compiler_params=_cparams(("parallel", "arbitrary")),
        name=name,
    )(*args)


def _headnorm_epilogue(acc, o_ref, extra_refs, j):
    gain_ref, = extra_refs
    for g in range(acc.shape[1] // LANES):
        sl = slice(g * LANES, (g + 1) * LANES)
        y = acc[:, sl]
        ms = jnp.mean(y * y, axis=-1, keepdims=True)
        o_ref[:, sl] = (y * lax.rsqrt(ms + NORM_EPS)) * gain_ref[:, sl]


def _rope_epilogue(acc, o_ref, extra_refs, j, *, head_dim, q_cols, k_scale):
    cos_ref, sin_ref = extra_refs
    cos = cos_ref[...]
    sin = sin_ref[...]
    half = head_dim // 2
    tn = acc.shape[1]
    for g in range(tn // head_dim):
        scale = jnp.where(j * tn + g * head_dim >= q_cols, jnp.float32(k_scale), jnp.float32(1.0))
        x1 = acc[:, g * head_dim:g * head_dim + half]
        x2 = acc[:, g * head_dim + half:(g + 1) * head_dim]
        o_ref[:, g * head_dim:g * head_dim + half] = (x1 * cos - x2 * sin) * scale
        o_ref[:, g * head_dim + half:(g + 1) * head_dim] = (x1 * sin + x2 * cos) * scale


def _softplus(z):
    return jnp.maximum(z, 0.0) + jnp.log1p(jnp.exp(-jnp.abs(z)))


def _suffix_sum_weights(n):
    row = lax.broadcasted_iota(jnp.int32, (n, 2 * n), 0)
    col = lax.broadcasted_iota(jnp.int32, (n, 2 * n), 1)
    return jnp.where((col >= n) | (row > col), 1.0, 0.0).astype(BF16)


def _split_dot(x, w):
    hi = x.astype(BF16)
    lo = (x - hi.astype(F32)).astype(BF16)
    return (jnp.dot(hi, w, preferred_element_type=F32)
            + jnp.dot(lo, w, preferred_element_type=F32))


def _sb_tile(z, valid, carry, lo_w):
    n = z.shape[1]
    sp = _softplus(z)
    log_beta = z - sp
    log_rest = -sp
    if valid is not None:
        log_rest = jnp.where(valid, log_rest, 0.0)
    cs = _split_dot(log_rest, lo_w)
    between = carry + cs[:, :n]
    a = jnp.exp(log_beta + between)
    if valid is not None:
        a = jnp.where(valid, a, 0.0)
    return a, carry + cs[:, n:]


def _sb_prompt_kernel(bias_ref, q_ref, k_ref, v_ref, o_ref, *, scale):
    i = pl.program_id(2)
    tq = q_ref.shape[0]
    q = q_ref[...].astype(BF16)
    bias = bias_ref[...]
    lo_w = _suffix_sum_weights(tq)
    row = lax.broadcasted_iota(jnp.int32, (tq, tq), 0)
    col = lax.broadcasted_iota(jnp.int32, (tq, tq), 1)

    def tile(kb, acc, carry, valid):
        rows = pl.ds(pl.multiple_of(kb * tq, tq), tq)
        k = k_ref[rows, :].astype(BF16)
        v = v_ref[rows, :].astype(BF16)
        z = lax.dot_general(q, k, (((1,), (1,)), ((), ())), preferred_element_type=F32)
        z = z * scale + bias
        a, carry = _sb_tile(z, valid, carry, lo_w)
        acc = acc + jnp.dot(a.astype(BF16), v, preferred_element_type=F32)
        return acc, carry

    zeros = jnp.zeros((tq, tq), F32)
    acc, carry = tile(i, jnp.zeros((tq, v_ref.shape[1]), F32), zeros, col < row)

    def body(jj, c):
        return tile(i - 1 - jj, c[0], c[1], None)

    acc, carry = lax.fori_loop(0, i, body, (acc, carry))
    o_ref[...] = acc.astype(o_ref.dtype)


def _sb_prompt(qk, v, bias_rep, *, batch, seq, heads):
    dh = LANES
    d = heads * dh
    nq = seq // Q_BLOCK
    kern = functools.partial(_sb_prompt_kernel, scale=dh ** -0.5)
    return pl.pallas_call(
        kern,
        grid=(batch, heads, nq),
        in_specs=[
            pl.BlockSpec((None, 1, LANES), lambda b, h, i: (h, 0, 0)),
            pl.BlockSpec((Q_BLOCK, dh), lambda b, h, i: (b * nq + i, h)),
            pl.BlockSpec((seq, dh), lambda b, h, i: (b, heads + h)),
            pl.BlockSpec((seq, dh), lambda b, h, i: (b, h)),
        ],
        out_specs=pl.BlockSpec((Q_BLOCK, dh), lambda b, h, i: (b * nq + i, h)),
        out_shape=jax.ShapeDtypeStruct((batch * seq, d), F32),
        compiler_params=_cparams(("parallel", "parallel", "arbitrary")),
        name="sb_prompt",
    )(bias_rep, qk, qk, v)


def _sb_sample_kernel(pt_ref, bias_ref, q_ref, kn_ref, vn_ref, kp_ref, vp_ref, o_ref,
                      acc_ref, carry_ref, *, heads, scale):
    s = pl.program_id(1)
    t = q_ref.shape[0]
    dh = LANES
    hq = heads * t
    bias = bias_ref[...]
    lo_w = _suffix_sum_weights(LANES)

    def q_head(h):
        qh = q_ref[:, h * dh:(h + 1) * dh]
        pad = jnp.zeros((BF16_ROWS - t, dh), F32)
        return jnp.concatenate([qh, pad], axis=0).astype(BF16)

    def attend(k_of, v_of, valid):
        zs = []
        for h in range(heads):
            zh = lax.dot_general(q_head(h), k_of(h), (((1,), (1,)), ((), ())),
                                 preferred_element_type=F32)
            zs.append(zh[:t])
        z = jnp.concatenate(zs, axis=0) * scale + bias
        a, carry = _sb_tile(z, valid, carry_ref[...], lo_w)
        carry_ref[...] = carry
        for h in range(heads):
            ah = a[h * t:(h + 1) * t]
            ah = jnp.concatenate([ah, jnp.zeros((BF16_ROWS - t, LANES), F32)], axis=0)
            oh = jnp.dot(ah.astype(BF16), v_of(h), preferred_element_type=F32)
            acc_ref[:, h * dh:(h + 1) * dh] += oh[:t]

    @pl.when(s == 0)
    def _():
        acc_ref[...] = jnp.zeros_like(acc_ref)
        carry_ref[...] = jnp.zeros_like(carry_ref)
        row = lax.broadcasted_iota(jnp.int32, (hq, LANES), 0)
        col = lax.broadcasted_iota(jnp.int32, (hq, LANES), 1)
        valid = col < lax.rem(row, t)
        zpad = jnp.zeros((LANES - t, dh), F32)

        def k_of(h):
            return jnp.concatenate([kn_ref[:, h * dh:(h + 1) * dh], zpad], axis=0).astype(BF16)

        def v_of(h):
            return jnp.concatenate([vn_ref[:, h * dh:(h + 1) * dh], zpad], axis=0).astype(BF16)

        attend(k_of, v_of, valid)

    @pl.when(s > 0)
    def _():
        def k_of(h):
            return kp_ref[pl.ds(h, PAGE_SIZE, stride=heads), :].astype(BF16)

        def v_of(h):
            return vp_ref[pl.ds(h, PAGE_SIZE, stride=heads), :].astype(BF16)

        attend(k_of, v_of, None)

    @pl.when(s == pl.num_programs(1) - 1)
    def _():
        o_ref[...] = acc_ref[...]


def _sb_sample(qk, v, cache_k, cache_v, page_table, bias_rows, *, row0, dec_batch, t, heads):
    dh = LANES
    d = heads * dh
    n_pages = page_table.shape[1]
    rb0 = row0 // t

    def page_map(b, s, pt):
        return (pt[b, jnp.clip(n_pages - s, 0, n_pages - 1)], 0, 0)

    kern = functools.partial(_sb_sample_kernel, heads=heads, scale=dh ** -0.5)
    grid_spec = pltpu.PrefetchScalarGridSpec(
        num_scalar_prefetch=1,
        grid=(dec_batch, n_pages + 1),
        in_specs=[
            pl.BlockSpec((heads * t, LANES), lambda b, s, pt: (0, 0)),
            pl.BlockSpec((t, d), lambda b, s, pt: (rb0 + b, 0)),
            pl.BlockSpec((t, d), lambda b, s, pt: (rb0 + b, 1)),
            pl.BlockSpec((t, d), lambda b, s, pt: (rb0 + b, 0)),
            pl.BlockSpec((None, PAGE_SIZE * heads, dh), page_map),
            pl.BlockSpec((None, PAGE_SIZE * heads, dh), page_map),
        ],
        out_specs=pl.BlockSpec((t, d), lambda b, s, pt: (b, 0)),
        scratch_shapes=[pltpu.VMEM((t, d), F32), pltpu.VMEM((heads * t, LANES), F32)],
    )
    return pl.pallas_call(
        kern,
        grid_spec=grid_spec,
        out_shape=jax.ShapeDtypeStruct((dec_batch * t, d), F32),
        compiler_params=_cparams(("parallel", "arbitrary")),
        name="sb_sample",
    )(page_table, bias_rows, qk, qk, v, cache_k, cache_v)


def _retention_kernel(lg_ref, gain_ref, q_ref, k_ref, v_ref, g_ref, s_ref, o_ref, so_ref, r_ref,
                      *, chunk_len):
    c = pl.program_id(2)
    rows_in = q_ref.shape[0]
    cp = LANES
    assert rows_in <= cp
    dk = q_ref.shape[1]
    dv = v_ref.shape[1]

    @pl.when(c == 0)
    def _():
        r_ref[...] = s_ref[...]

    def load(ref):
        x = ref[...]
        if rows_in < cp:
            x = jnp.concatenate([x, jnp.zeros((cp - rows_in, x.shape[1]), F32)], axis=0)
        return x

    lg = lg_ref[...]
    row = lax.broadcasted_iota(jnp.int32, (cp, LANES), 0).astype(F32)
    col = lax.broadcasted_iota(jnp.int32, (cp, LANES), 1).astype(F32)
    decay_query = jnp.exp((row + 1.0) * lg)
    decay_key = jnp.exp((chunk_len - 1.0 - row) * lg)
    decay_chunk = jnp.exp(chunk_len * lg)
    diff = row - col
    decay_intra = jnp.where(diff >= 0, jnp.exp(jnp.maximum(diff, 0.0) * lg), 0.0)

    q = load(q_ref)
    k = load(k_ref)
    v = load(v_ref).astype(BF16)
    qb = q.astype(BF16)
    kb = k.astype(BF16)
    att = lax.dot_general(qb, kb, (((1,), (1,)), ((), ())), preferred_element_type=F32)
    att = att * decay_intra
    inner = jnp.dot(att.astype(BF16), v, preferred_element_type=F32)
    r = r_ref[...]
    cross = jnp.dot(qb, r.astype(BF16), preferred_element_type=F32)
    kd = jnp.concatenate([k[:, s * LANES:(s + 1) * LANES] * decay_key
                          for s in range(dk // LANES)], axis=1).astype(BF16)
    upd = lax.dot_general(kd, v, (((0,), (0,)), ((), ())), preferred_element_type=F32)
    for s in range(dv // LANES):
        sl = slice(s * LANES, (s + 1) * LANES)
        r_ref[:, sl] = decay_chunk * r[:, sl] + upd[:, sl]
    out = jnp.concatenate([inner[:, s * LANES:(s + 1) * LANES]
                           + cross[:, s * LANES:(s + 1) * LANES] * decay_query
                           for s in range(dv // LANES)], axis=1)[:rows_in]
    ms = jnp.mean(out * out, axis=-1, keepdims=True)
    o = (out * lax.rsqrt(ms + NORM_EPS)) * gain_ref[...]
    g = g_ref[...]
    o_ref[...] = (g * jax.nn.sigmoid(g)) * o

    @pl.when(c == pl.num_programs(2) - 1)
    def _():
        so_ref[...] = r_ref[...]


def _retention(qk, vg, state, lg_rep, gain, *, row0, batch, seq, heads):
    dk = qk.shape[1] // (2 * heads)
    dv = vg.shape[1] // (2 * heads)
    assert dk % LANES == 0 and dv % LANES == 0
    c = RET_CHUNK if seq % RET_CHUNK == 0 else seq
    nc = seq // c
    rb0 = row0 // c

    def rows(b, h, ci):
        return rb0 + b * nc + ci

    kern = functools.partial(_retention_kernel, chunk_len=float(c))
    return pl.pallas_call(
        kern,
        grid=(batch, heads, nc),
        in_specs=[
            pl.BlockSpec((None, 1, LANES), lambda b, h, ci: (h, 0, 0)),
            pl.BlockSpec((None, 1, dv), lambda b, h, ci: (h, 0, 0)),
            pl.BlockSpec((c, dk), lambda b, h, ci: (rows(b, h, ci), h)),
            pl.BlockSpec((c, dk), lambda b, h, ci: (rows(b, h, ci), heads + h)),
            pl.BlockSpec((c, dv), lambda b, h, ci: (rows(b, h, ci), h)),
            pl.BlockSpec((c, dv), lambda b, h, ci: (rows(b, h, ci), heads + h)),
            pl.BlockSpec((None, None, dk, dv), lambda b, h, ci: (b, h, 0, 0)),
        ],
        out_specs=[
            pl.BlockSpec((c, dv), lambda b, h, ci: (b * nc + ci, h)),
            pl.BlockSpec((None, None, dk, dv), lambda b, h, ci: (b, h, 0, 0)),
        ],
        out_shape=[
            jax.ShapeDtypeStruct((batch * seq, heads * dv), F32),
            jax.ShapeDtypeStruct((batch, heads, dk, dv), F32),
        ],
        scratch_shapes=[pltpu.VMEM((dk, dv), F32)],
        compiler_params=_cparams(("parallel", "parallel", "arbitrary")),
        name="retention",
    )(lg_rep, gain, qk, qk, vg, vg, state)


def _topk_rows(s, k, payload=None):
    r_, t_ = s.shape
    rows = lax.broadcasted_iota(jnp.int32, (r_, t_), 0)
    krow = lax.broadcasted_iota(jnp.int32, (k, t_), 0)

    def body(r, carry):
        s, vals, idxs = carry
        m = jnp.max(s, axis=0, keepdims=True)
        am = jnp.min(jnp.where(s == m, rows, r_), axis=0, keepdims=True)
        hit = rows == am
        if payload is None:
            pv = am
        else:
            pv = jnp.max(jnp.where(hit, payload, -1), axis=0, keepdims=True)
        vals = jnp.where(krow == r, m, vals)
        idxs = jnp.where(krow == r, pv, idxs)
        s = jnp.where(hit, -jnp.inf, s)
        return s, vals, idxs

    init = (s, jnp.zeros((k, t_), F32), jnp.zeros((k, t_), jnp.int32))
    _, vals, idxs = lax.fori_loop(0, k, body, init)
    return vals, idxs


def _peer_route_kernel(x_ref, g_ref, wq_ref, keys_ref, xn_ref, idx_ref, gate_ref):
    x = x_ref[...]
    ms = jnp.mean(x * x, axis=-1, keepdims=True)
    xn = (x * lax.rsqrt(ms + NORM_EPS)) * g_ref[...]
    xn_ref[...] = xn
    q = jnp.dot(xn.astype(BF16), wq_ref[...], preferred_element_type=F32)
    st = lax.dot_general(keys_ref[...], q.astype(BF16), (((1,), (1,)), ((), ())),
                         preferred_element_type=F32)
    kk = PEER_TOPK
    for h in range(PEER_HEADS):
        base = h * 2 * N_KEYS
        v1, i1 = _topk_rows(st[base:base + N_KEYS], kk)
        v2, i2 = _topk_rows(st[base + N_KEYS:base + 2 * N_KEYS], kk)
        cand = jnp.concatenate([v1[a:a + 1] + v2 for a in range(kk)], axis=0)
        cidx = jnp.concatenate([i1[a:a + 1] * N_KEYS + i2 for a in range(kk)], axis=0)
        sc, idx = _topk_rows(cand, kk, payload=cidx)
        e = jnp.exp(sc - sc[0:1])
        gate = e / jnp.sum(e, axis=0, keepdims=True)
        idx_ref[h * kk:(h + 1) * kk, :] = idx
        gate_ref[h * kk:(h + 1) * kk, :] = gate


def _peer_route(h, gain, wq, keys_bd, *, tb=128):
    n, d = h.shape
    hk = PEER_HEADS * PEER_TOPK
    return pl.pallas_call(
        _peer_route_kernel,
        grid=(n // tb,),
        in_specs=[
            pl.BlockSpec((tb, d), lambda i: (i, 0)),
            pl.BlockSpec((1, d), lambda i: (0, 0)),
            pl.BlockSpec(wq.shape, lambda i: (0, 0)),
            pl.BlockSpec(keys_bd.shape, lambda i: (0, 0)),
        ],
        out_specs=[
            pl.BlockSpec((tb, d), lambda i: (i, 0)),
            pl.BlockSpec((hk, tb), lambda i: (0, i)),
            pl.BlockSpec((hk, tb), lambda i: (0, i)),
        ],
        out_shape=[
            jax.ShapeDtypeStruct((n, d), F32),
            jax.ShapeDtypeStruct((hk, n), jnp.int32),
            jax.ShapeDtypeStruct((hk, n), F32),
        ],
        compiler_params=_cparams(("parallel",)),
        name="peer_route",
    )(h, gain.reshape(1, d).astype(F32), wq, keys_bd)


def _gelu_tanh(x):
    c = 0.7978845608028654
    return 0.5 * x * (1.0 + jnp.tanh(c * (x + 0.044715 * (x * x * x))))


def _pad_rows(x, rows):
    if x.shape[0] >= rows:
        return x
    return jnp.concatenate([x, jnp.zeros((rows - x.shape[0], x.shape[1]), x.dtype)], axis=0)


def _peer_expert_kernel(idx_ref, idx_next_ref, gate_ref, xn_ref, h_ref, tab_ref, o_ref,
                        buf_ref, sem_ref, *, d):
    i = pl.program_id(0)
    n_steps = pl.num_programs(0)
    tb, hk = gate_ref.shape
    pairs = tb * hk
    slot = i % 2

    def issue(ids_ref, dst_slot):
        for t in range(tb):
            def body(j, carry):
                e = ids_ref[t, j]
                pltpu.make_async_copy(tab_ref.at[pl.ds(e, 1), :],
                                      buf_ref.at[dst_slot, pl.ds(t * hk + j, 1), :],
                                      sem_ref.at[dst_slot]).start()
                return carry
            lax.fori_loop(0, hk, body, 0, unroll=8)

    @pl.when(i == 0)
    def _():
        issue(idx_ref, 0)

    @pl.when(i + 1 < n_steps)
    def _():
        issue(idx_next_ref, 1 - slot)

    pltpu.make_async_copy(buf_ref.at[slot], buf_ref.at[slot], sem_ref.at[slot]).wait()

    x = _pad_rows(xn_ref[...], BF16_ROWS).astype(BF16)
    u_rows = buf_ref[slot, :, 0:d].astype(BF16)
    act = lax.dot_general(x, u_rows, (((1,), (1,)), ((), ())),
                          preferred_element_type=F32)[:tb]
    gate = jnp.concatenate([gate_ref[...]] * tb, axis=1)
    row = lax.broadcasted_iota(jnp.int32, (tb, pairs), 0)
    col = lax.broadcasted_iota(jnp.int32, (tb, pairs), 1)
    own = (col >= row * hk) & (col < (row + 1) * hk)
    coef = jnp.where(own, gate * _gelu_tanh(act), 0.0)
    v_rows = buf_ref[slot, :, d:2 * d].astype(BF16)
    out = jnp.dot(_pad_rows(coef, BF16_ROWS).astype(BF16), v_rows,
                  preferred_element_type=F32)[:tb]
    o_ref[...] = h_ref[...] + out


def _peer_experts(idx, gate, xn, h, table):
    n, d = h.shape
    hk = idx.shape[1]
    tb = PEER_TOKENS_PER_STEP
    n_steps = n // tb
    kern = functools.partial(_peer_expert_kernel, d=d)
    return pl.pallas_call(
        kern,
        grid=(n_steps,),
        in_specs=[
            pl.BlockSpec((tb, hk), lambda i: (i, 0), memory_space=pltpu.SMEM),
            pl.BlockSpec((tb, hk), lambda i: (jnp.minimum(i + 1, n_steps - 1), 0),
                         memory_space=pltpu.SMEM),
            pl.BlockSpec((tb, hk), lambda i: (i, 0)),
            pl.BlockSpec((tb, d), lambda i: (i, 0)),
            pl.BlockSpec((tb, d), lambda i: (i, 0)),
            pl.BlockSpec(memory_space=pl.ANY),
        ],
        out_specs=pl.BlockSpec((tb, d), lambda i: (i, 0)),
        out_shape=jax.ShapeDtypeStruct((n, d), F32),
        scratch_shapes=[pltpu.VMEM((2, tb * hk, 2 * d), F32), pltpu.SemaphoreType.DMA((2,))],
        compiler_params=_cparams(("arbitrary",)),
        name="peer_experts",
    )(idx, idx, gate, xn, h, table)


def _peer_keys_blockdiag(keys1, keys2):
    half = PEER_KEY_DIM // 2
    eye = jnp.eye(PEER_HEADS * 2, dtype=F32)
    keys = jnp.stack([keys1, keys2]).astype(F32)
    keys = jnp.tile(keys, (PEER_HEADS, 1, 1))
    bd = jnp.einsum("gnc,gf->gnfc", keys, eye)
    return bd.reshape(PEER_HEADS * 2 * N_KEYS, PEER_HEADS * 2 * half).astype(BF16)


def _peer_layer(h, gain, wq, keys1, keys2, u, v):
    xn, idx_t, gate_t = _peer_route(h, gain, wq.astype(BF16), _peer_keys_blockdiag(keys1, keys2))
    table = jnp.concatenate([u, v], axis=1)
    return _peer_experts(idx_t.T, gate_t.T, xn, h, table)


def _rope_tables(positions, head_dim):
    half = head_dim // 2
    inv = ROPE_BASE ** (-jnp.arange(half, dtype=F32) / half)
    ang = positions.astype(F32)[:, None] * inv[None, :]
    return jnp.cos(ang), jnp.sin(ang)


def kernel(x_prompt, x_sample, cache_k, cache_v, state_ret, page_table, norm_mix, norm_ffn,
           sb_wqkv, sb_q_gain, sb_k_gain, sb_bias, sb_wo, ret_wqkvg, ret_norm_gain, ret_wo,
           peer_wq, peer_keys1, peer_keys2, peer_u, peer_v):
    bp, sp, d = x_prompt.shape
    bd, sd, _ = x_sample.shape
    n_p = bp * sp
    n_s = bd * sd
    n_pages = page_table.shape[1]
    past = n_pages * PAGE_SIZE
    dh = d // SB_HEADS
    n_pool = cache_k.shape[1]

    h = jnp.concatenate([x_prompt.reshape(n_p, d), x_sample.reshape(n_s, d)], axis=0)

    wqkv = sb_wqkv[0].astype(BF16)
    qk_gain = jnp.concatenate([jnp.tile(sb_q_gain[0], SB_HEADS),
                               jnp.tile(sb_k_gain[0], SB_HEADS)]).reshape(1, 2 * d).astype(F32)
    qk = _norm_matmul(h, wqkv[:, :2 * d], gain=norm_mix[0], epilogue=_headnorm_epilogue,
                      extras=(qk_gain,),
                      extra_specs=(lambda tm, tn: pl.BlockSpec((1, tn), lambda i, j: (0, j)),),
                      name="sb_qk_proj")
    v = _norm_matmul(h, wqkv[:, 2 * d:], gain=norm_mix[0], name="sb_v_proj")

    bias = sb_bias[0].astype(F32)
    bias_rep = jnp.broadcast_to(bias[:, None, None], (SB_HEADS, 1, LANES))
    att_p = _sb_prompt(qk, v, bias_rep, batch=bp, seq=sp, heads=SB_HEADS)
    bias_rows = jnp.broadcast_to(jnp.repeat(bias, sd)[:, None], (SB_HEADS * sd, LANES))
    att_s = _sb_sample(qk, v,
                       cache_k[0].reshape(n_pool, PAGE_SIZE * SB_HEADS, dh),
                       cache_v[0].reshape(n_pool, PAGE_SIZE * SB_HEADS, dh),
                       page_table, bias_rows, row0=n_p, dec_batch=bd, t=sd, heads=SB_HEADS)
    att = jnp.concatenate([att_p, att_s], axis=0)
    h = _norm_matmul(att, sb_wo[0].astype(BF16), res=h, name="sb_wo")
    h = _peer_layer(h, norm_ffn[0], peer_wq[0], peer_keys1[0], peer_keys2[0], peer_u[0], peer_v[0])

    k_all = qk[:, d:]
    new_k_prompt = k_all[:n_p].reshape(1, bp, sp, SB_HEADS, dh)
    new_k_sample = k_all[n_p:].reshape(1, bd, sd, SB_HEADS, dh)
    new_v_prompt = v[:n_p].reshape(1, bp, sp, SB_HEADS, dh)
    new_v_sample = v[n_p:].reshape(1, bd, sd, SB_HEADS, dh)

    dk = d // RET_HEADS
    dv = 2 * d // RET_HEADS
    hq = RET_HEADS * dk
    hv = RET_HEADS * dv
    w = ret_wqkvg[0].astype(BF16)
    pos = jnp.concatenate([jnp.tile(jnp.arange(sp), bp), jnp.tile(past + jnp.arange(sd), bd)])
    cos, sin = _rope_tables(pos, dk)
    rope = functools.partial(_rope_epilogue, head_dim=dk, q_cols=hq, k_scale=dk ** -0.5)

    def half_spec(tm, tn):
        return pl.BlockSpec((tm, dk // 2), lambda i, j: (i, 0))

    rqk = _norm_matmul(h, w[:, :2 * hq], gain=norm_mix[1], epilogue=rope, extras=(cos, sin),
                       extra_specs=(half_spec, half_spec), name="ret_qk_proj")
    rvg = _norm_matmul(h, w[:, 2 * hq:], gain=norm_mix[1], name="ret_vg_proj")

    log_g = jnp.log1p(-jnp.exp2(-5.0 - jnp.arange(RET_HEADS, dtype=F32)))
    lg_rep = jnp.broadcast_to(log_g[:, None, None], (RET_HEADS, 1, LANES))
    ret_gain = ret_norm_gain[0].reshape(RET_HEADS, 1, dv).astype(F32)
    zero_state = jnp.zeros((bp, RET_HEADS, dk, dv), F32)
    o_p, r_p = _retention(rqk, rvg, zero_state, lg_rep, ret_gain,
                          row0=0, batch=bp, seq=sp, heads=RET_HEADS)
    o_s, r_s = _retention(rqk, rvg, state_ret[0], lg_rep, ret_gain,
                          row0=n_p, batch=bd, seq=sd, heads=RET_HEADS)
    o = jnp.concatenate([o_p, o_s], axis=0)
    h = _norm_matmul(o, ret_wo[0].astype(BF16), res=h, name="ret_wo")
    h = _peer_layer(h, norm_ffn[1], peer_wq[1], peer_keys1[1], peer_keys2[1], peer_u[1], peer_v[1])

    return (h[:n_p].reshape(bp, sp, d), h[n_p:].reshape(bd, sd, d),
            new_k_prompt, new_v_prompt, new_k_sample, new_v_sample,
            r_p[None], r_s[None])
```

```python
import functools

import jax
import jax.numpy as jnp
from jax import lax
from jax.experimental import pallas as pl
from jax.experimental.pallas import tpu as pltpu

F32 = jnp.float32
BF16 = jnp.bfloat16

SB_HEADS = 16
RET_HEADS = 8
PEER_HEADS = 8
PEER_KEY_DIM = 128
N_KEYS = 128
PEER_TOPK = 16
PAGE_SIZE = 128
Q_BLOCK = 128
RET_CHUNK = 128
ROPE_BASE = 10000.0
NORM_EPS = 1e-6

LANES = 128
SUBLANES = 8
BF16_ROWS = 16
VMEM_LIMIT = 52 * 1024 * 1024

PEER_TOKENS_PER_STEP = 8
PEER_PITCH_PAD = 4
PEER_VMEM_LIMIT = 58 * 1024 * 1024
SB_PROMPT_Q_ROWS = 512


def _cparams(sem, vmem=VMEM_LIMIT):
    return pltpu.CompilerParams(dimension_semantics=sem, vmem_limit_bytes=vmem)


def _mm_kernel(*refs, has_gain, has_res, n_extra, epilogue, row_chunk):
    it = iter(refs)
    x_ref = next(it)
    g_ref = next(it) if has_gain else None
    w_ref = next(it)
    r_ref = next(it) if has_res else None
    extra_refs = [next(it) for _ in range(n_extra)]
    o_ref = next(it)
    xn_ref = next(it)
    tm = x_ref.shape[0]

    @pl.when(pl.program_id(1) == 0)
    def _():
        def chunk(c, carry):
            rows = pl.ds(pl.multiple_of(c * row_chunk, row_chunk), row_chunk)
            xv = x_ref[rows, :].astype(F32)
            if has_gain:
                ms = jnp.mean(xv * xv, axis=-1, keepdims=True)
                xv = (xv * lax.rsqrt(ms + NORM_EPS)) * g_ref[...]
            xn_ref[rows, :] = xv.astype(BF16)
            return carry
        lax.fori_loop(0, tm // row_chunk, chunk, 0)

    acc = jnp.dot(xn_ref[...], w_ref[...], preferred_element_type=F32)
    if has_res:
        acc = acc + r_ref[...]
    if epilogue is None:
        o_ref[...] = acc.astype(o_ref.dtype)
    else:
        epilogue(acc, o_ref, extra_refs, pl.program_id(1))


def _norm_matmul(x, w, *, gain=None, res=None, epilogue=None, extras=(), extra_specs=(),
                 tm=512, tn=512, name="mm"):
    n, k = x.shape
    m = w.shape[1]
    tn = min(tn, m)
    assert n % tm == 0 and m % tn == 0, (n, tm, m, tn)
    extra_specs = [make(tm, tn) for make in extra_specs]
    in_specs = [pl.BlockSpec((tm, k), lambda i, j: (i, 0))]
    args = [x]
    if gain is not None:
        in_specs.append(pl.BlockSpec((1, k), lambda i, j: (0, 0)))
        args.append(gain.reshape(1, k).astype(F32))
    in_specs.append(pl.BlockSpec((k, tn), lambda i, j: (0, j)))
    args.append(w)
    if res is not None:
        in_specs.append(pl.BlockSpec((tm, tn), lambda i, j: (i, j)))
        args.append(res)
    in_specs.extend(extra_specs)
    args.extend(extras)
    kern = functools.partial(_mm_kernel, has_gain=gain is not None, has_res=res is not None,
                             n_extra=len(extras), epilogue=epilogue, row_chunk=min(64, tm))
    return pl.pallas_call(
        kern,
        grid=(n // tm, m // tn),
        in_specs=in_specs,
        out_specs=pl.BlockSpec((tm, tn), lambda i, j: (i, j)),
        out_shape=jax.ShapeDtypeStruct((n, m), F32),
        scratch_shapes=[pltpu.VMEM((tm, k), BF16)],
        compiler_params=_cparams(("parallel", "arbitrary")),
        name=name,
    )(*args)


def _headnorm_epilogue(acc, o_ref, extra_refs, j):
    gain_ref, = extra_refs
    for g in range(acc.shape[1] // LANES):
        sl = slice(g * LANES, (g + 1) * LANES)
        y = acc[:, sl]
        ms = jnp.mean(y * y, axis=-1, keepdims=True)
        o_ref[:, sl] = (y * lax.rsqrt(ms + NORM_EPS)) * gain_ref[:, sl]


def _rope_epilogue(acc, o_ref, extra_refs, j, *, head_dim, q_cols, k_scale):
    cos_ref, sin_ref = extra_refs
    cos = cos_ref[...]
    sin = sin_ref[...]
    half = head_dim // 2
    tn = acc.shape[1]
    for g in range(tn // head_dim):
        scale = jnp.where(j * tn + g * head_dim >= q_cols, jnp.float32(k_scale), jnp.float32(1.0))
        x1 = acc[:, g * head_dim:g * head_dim + half]
        x2 = acc[:, g * head_dim + half:(g + 1) * head_dim]
        o_ref[:, g * head_dim:g * head_dim + half] = (x1 * cos - x2 * sin) * scale
        o_ref[:, g * head_dim + half:(g + 1) * head_dim] = (x1 * sin + x2 * cos) * scale


def _softplus(z):
    return jnp.maximum(z, 0.0) + jnp.log1p(jnp.exp(-jnp.abs(z)))


def _suffix_sum_weights(n):
    row = lax.broadcasted_iota(jnp.int32, (n, 2 * n), 0)
    col = lax.broadcasted_iota(jnp.int32, (n, 2 * n), 1)
    return jnp.where((col >= n) | (row > col), 1.0, 0.0).astype(BF16)


def _split_dot(x, w):
    hi = x.astype(BF16)
    lo = (x - hi.astype(F32)).astype(BF16)
    return (jnp.dot(hi, w, preferred_element_type=F32)
            + jnp.dot(lo, w, preferred_element_type=F32))


def _sb_tile(z, valid, carry, lo_w):
    n = z.shape[1]
    sp = _softplus(z)
    log_beta = z - sp
    log_rest = -sp
    if valid is not None:
        log_rest = jnp.where(valid, log_rest, 0.0)
    cs = _split_dot(log_rest, lo_w)
    between = carry + cs[:, :n]
    a = jnp.exp(log_beta + between)
    if valid is not None:
        a = jnp.where(valid, a, 0.0)
    return a, carry + cs[:, n:]


def _sb_prompt_kernel(bias_ref, q_ref, k_ref, v_ref, o_ref, *, scale):
    i = pl.program_id(2)
    tq = q_ref.shape[0]
    tk = LANES
    sub = tq // tk
    q = q_ref[...].astype(BF16)
    bias = bias_ref[...]
    lo_w = _suffix_sum_weights(tk)
    row = lax.broadcasted_iota(jnp.int32, (tq, tk), 0)
    col = lax.broadcasted_iota(jnp.int32, (tq, tk), 1)

    def tile(kt, acc, carry, valid):
        rows = pl.ds(pl.multiple_of(kt * tk, tk), tk)
        k = k_ref[rows, :].astype(BF16)
        v = v_ref[rows, :].astype(BF16)
        z = lax.dot_general(q, k, (((1,), (1,)), ((), ())), preferred_element_type=F32)
        z = z * scale + bias
        a, carry = _sb_tile(z, valid, carry, lo_w)
        acc = acc + jnp.dot(a.astype(BF16), v, preferred_element_type=F32)
        return acc, carry

    acc = jnp.zeros((tq, v_ref.shape[1]), F32)
    carry = jnp.zeros((tq, tk), F32)
    for dgl in reversed(range(sub)):
        acc, carry = tile(i * sub + dgl, acc, carry, col + dgl * tk < row)

    pair = 2 if sub % 2 == 0 else 1

    def body(jj, c):
        acc, carry = c
        for u in range(pair):
            acc, carry = tile(i * sub - 1 - pair * jj - u, acc, carry, None)
        return acc, carry

    acc, carry = lax.fori_loop(0, (i * sub) // pair, body, (acc, carry))
    o_ref[...] = acc.astype(o_ref.dtype)


def _sb_prompt(qk, v, bias_rep, *, batch, seq, heads):
    dh = LANES
    d = heads * dh
    tq = SB_PROMPT_Q_ROWS if seq % SB_PROMPT_Q_ROWS == 0 else LANES
    nq = seq // tq
    kern = functools.partial(_sb_prompt_kernel, scale=dh ** -0.5)
    return pl.pallas_call(
        kern,
        grid=(batch, heads, nq),
        in_specs=[
            pl.BlockSpec((None, 1, LANES), lambda b, h, i: (h, 0, 0)),
            pl.BlockSpec((tq, dh), lambda b, h, i: (b * nq + i, h)),
            pl.BlockSpec((seq, dh), lambda b, h, i: (b, heads + h)),
            pl.BlockSpec((seq, dh), lambda b, h, i: (b, h)),
        ],
        out_specs=pl.BlockSpec((tq, dh), lambda b, h, i: (b * nq + i, h)),
        out_shape=jax.ShapeDtypeStruct((batch * seq, d), F32),
        compiler_params=_cparams(("parallel", "parallel", "arbitrary")),
        name="sb_prompt",
    )(bias_rep, qk, qk, v)


def _sb_sample_kernel(pt_ref, bias_ref, q_ref, kn_ref, vn_ref, kp_ref, vp_ref, o_ref,
                      acc_ref, carry_ref, *, heads, scale):
    s = pl.program_id(1)
    t = q_ref.shape[0]
    dh = LANES
    hq = heads * t
    bias = bias_ref[...]
    lo_w = _suffix_sum_weights(LANES)

    def q_head(h):
        qh = q_ref[:, h * dh:(h + 1) * dh]
        pad = jnp.zeros((BF16_ROWS - t, dh), F32)
        return jnp.concatenate([qh, pad], axis=0).astype(BF16)

    def attend(k_of, v_of, valid):
        zs = []
        for h in range(heads):
            zh = lax.dot_general(q_head(h), k_of(h), (((1,), (1,)), ((), ())),
                                 preferred_element_type=F32)
            zs.append(zh[:t])
        z = jnp.concatenate(zs, axis=0) * scale + bias
        a, carry = _sb_tile(z, valid, carry_ref[...], lo_w)
        carry_ref[...] = carry
        for h in range(heads):
            ah = a[h * t:(h + 1) * t]
            ah = jnp.concatenate([ah, jnp.zeros((BF16_ROWS - t, LANES), F32)], axis=0)
            oh = jnp.dot(ah.astype(BF16), v_of(h), preferred_element_type=F32)
            acc_ref[:, h * dh:(h + 1) * dh] += oh[:t]

    @pl.when(s == 0)
    def _():
        acc_ref[...] = jnp.zeros_like(acc_ref)
        carry_ref[...] = jnp.zeros_like(carry_ref)
        row = lax.broadcasted_iota(jnp.int32, (hq, LANES), 0)
        col = lax.broadcasted_iota(jnp.int32, (hq, LANES), 1)
        valid = col < lax.rem(row, t)
        zpad = jnp.zeros((LANES - t, dh), F32)

        def k_of(h):
            return jnp.concatenate([kn_ref[:, h * dh:(h + 1) * dh], zpad], axis=0).astype(BF16)

        def v_of(h):
            return jnp.concatenate([vn_ref[:, h * dh:(h + 1) * dh], zpad], axis=0).astype(BF16)

        attend(k_of, v_of, valid)

    @pl.when(s > 0)
    def _():
        def k_of(h):
            return kp_ref[pl.ds(h, PAGE_SIZE, stride=heads), :].astype(BF16)

        def v_of(h):
            return vp_ref[pl.ds(h, PAGE_SIZE, stride=heads), :].astype(BF16)

        attend(k_of, v_of, None)

    @pl.when(s == pl.num_programs(1) - 1)
    def _():
        o_ref[...] = acc_ref[...]


def _sb_sample(qk, v, cache_k, cache_v, page_table, bias_rows, *, row0, dec_batch, t, heads):
    dh = LANES
    d = heads * dh
    n_pages = page_table.shape[1]
    rb0 = row0 // t

    def page_map(b, s, pt):
        return (pt[b, jnp.clip(n_pages - s, 0, n_pages - 1)], 0, 0)

    kern = functools.partial(_sb_sample_kernel, heads=heads, scale=dh ** -0.5)
    grid_spec = pltpu.PrefetchScalarGridSpec(
        num_scalar_prefetch=1,
        grid=(dec_batch, n_pages + 1),
        in_specs=[
            pl.BlockSpec((heads * t, LANES), lambda b, s, pt: (0, 0)),
            pl.BlockSpec((t, d), lambda b, s, pt: (rb0 + b, 0)),
            pl.BlockSpec((t, d), lambda b, s, pt: (rb0 + b, 1)),
            pl.BlockSpec((t, d), lambda b, s, pt: (rb0 + b, 0)),
            pl.BlockSpec((None, PAGE_SIZE * heads, dh), page_map),
            pl.BlockSpec((None, PAGE_SIZE * heads, dh), page_map),
        ],
        out_specs=pl.BlockSpec((t, d), lambda b, s, pt: (b, 0)),
        scratch_shapes=[pltpu.VMEM((t, d), F32), pltpu.VMEM((heads * t, LANES), F32)],
    )
    return pl.pallas_call(
        kern,
        grid_spec=grid_spec,
        out_shape=jax.ShapeDtypeStruct((dec_batch * t, d), F32),
        compiler_params=_cparams(("parallel", "arbitrary")),
        name="sb_sample",
    )(page_table, bias_rows, qk, qk, v, cache_k, cache_v)


def _retention_kernel(lg_ref, gain_ref, q_ref, k_ref, v_ref, g_ref, s_ref, o_ref, so_ref, r_ref,
                      *, chunk_len):
    c = pl.program_id(2)
    rows_in = q_ref.shape[0]
    cp = LANES
    assert rows_in <= cp
    dk = q_ref.shape[1]
    dv = v_ref.shape[1]

    @pl.when(c == 0)
    def _():
        r_ref[...] = s_ref[...]

    def load(ref):
        x = ref[...]
        if rows_in < cp:
            x = jnp.concatenate([x, jnp.zeros((cp - rows_in, x.shape[1]), F32)], axis=0)
        return x

    lg = lg_ref[...]
    row = lax.broadcasted_iota(jnp.int32, (cp, LANES), 0).astype(F32)
    col = lax.broadcasted_iota(jnp.int32, (cp, LANES), 1).astype(F32)
    decay_query = jnp.exp((row + 1.0) * lg)
    decay_key = jnp.exp((chunk_len - 1.0 - row) * lg)
    decay_chunk = jnp.exp(chunk_len * lg)
    diff = row - col
    decay_intra = jnp.where(diff >= 0, jnp.exp(jnp.maximum(diff, 0.0) * lg), 0.0)

    q = load(q_ref)
    k = load(k_ref)
    v = load(v_ref).astype(BF16)
    qb = q.astype(BF16)
    kb = k.astype(BF16)
    att = lax.dot_general(qb, kb, (((1,), (1,)), ((), ())), preferred_element_type=F32)
    att = att * decay_intra
    inner = jnp.dot(att.astype(BF16), v, preferred_element_type=F32)
    r = r_ref[...]
    cross = jnp.dot(qb, r.astype(BF16), preferred_element_type=F32)
    kd = jnp.concatenate([k[:, s * LANES:(s + 1) * LANES] * decay_key
                          for s in range(dk // LANES)], axis=1).astype(BF16)
    upd = lax.dot_general(kd, v, (((0,), (0,)), ((), ())), preferred_element_type=F32)
    for s in range(dv // LANES):
        sl = slice(s * LANES, (s + 1) * LANES)
        r_ref[:, sl] = decay_chunk * r[:, sl] + upd[:, sl]
    out = jnp.concatenate([inner[:, s * LANES:(s + 1) * LANES]
                           + cross[:, s * LANES:(s + 1) * LANES] * decay_query
                           for s in range(dv // LANES)], axis=1)[:rows_in]
    ms = jnp.mean(out * out, axis=-1, keepdims=True)
    o = (out * lax.rsqrt(ms + NORM_EPS)) * gain_ref[...]
    g = g_ref[...]
    o_ref[...] = (g * jax.nn.sigmoid(g)) * o

    @pl.when(c == pl.num_programs(2) - 1)
    def _():
        so_ref[...] = r_ref[...]


def _retention(qk, vg, state, lg_rep, gain, *, row0, batch, seq, heads):
    dk = qk.shape[1] // (2 * heads)
    dv = vg.shape[1] // (2 * heads)
    assert dk % LANES == 0 and dv % LANES == 0
    c = RET_CHUNK if seq % RET_CHUNK == 0 else seq
    nc = seq // c
    rb0 = row0 // c

    def rows(b, h, ci):
        return rb0 + b * nc + ci

    kern = functools.partial(_retention_kernel, chunk_len=float(c))
    return pl.pallas_call(
        kern,
        grid=(batch, heads, nc),
        in_specs=[
            pl.BlockSpec((None, 1, LANES), lambda b, h, ci: (h, 0, 0)),
            pl.BlockSpec((None, 1, dv), lambda b, h, ci: (h, 0, 0)),
            pl.BlockSpec((c, dk), lambda b, h, ci: (rows(b, h, ci), h)),
            pl.BlockSpec((c, dk), lambda b, h, ci: (rows(b, h, ci), heads + h)),
            pl.BlockSpec((c, dv), lambda b, h, ci: (rows(b, h, ci), h)),
            pl.BlockSpec((c, dv), lambda b, h, ci: (rows(b, h, ci), heads + h)),
            pl.BlockSpec((None, None, dk, dv), lambda b, h, ci: (b, h, 0, 0)),
        ],
        out_specs=[
            pl.BlockSpec((c, dv), lambda b, h, ci: (b * nc + ci, h)),
            pl.BlockSpec((None, None, dk, dv), lambda b, h, ci: (b, h, 0, 0)),
        ],
        out_shape=[
            jax.ShapeDtypeStruct((batch * seq, heads * dv), F32),
            jax.ShapeDtypeStruct((batch, heads, dk, dv), F32),
        ],
        scratch_shapes=[pltpu.VMEM((dk, dv), F32)],
        compiler_params=_cparams(("parallel", "parallel", "arbitrary")),
        name="retention",
    )(lg_rep, gain, qk, qk, vg, vg, state)


def _topk_rows(problems, k):
    t_ = problems[0][0].shape[1]
    krow = lax.broadcasted_iota(jnp.int32, (k, t_), 0)
    rows = [lax.broadcasted_iota(jnp.int32, s.shape, 0) for s, _ in problems]

    def body(r, carry):
        out = []
        for (s, vals, idxs), (_, payload), rw in zip(carry, problems, rows):
            m = jnp.max(s, axis=0, keepdims=True)
            am = jnp.min(jnp.where(s == m, rw, s.shape[0]), axis=0, keepdims=True)
            hit = rw == am
            if payload is None:
                pv = am
            else:
                pv = jnp.max(jnp.where(hit, payload, -1), axis=0, keepdims=True)
            vals = jnp.where(krow == r, m, vals)
            idxs = jnp.where(krow == r, pv, idxs)
            out.append((jnp.where(hit, -jnp.inf, s), vals, idxs))
        return tuple(out)

    init = tuple((s, jnp.zeros((k, t_), F32), jnp.zeros((k, t_), jnp.int32)) for s, _ in problems)
    res = lax.fori_loop(0, k, body, init)
    return [(vals, idxs) for _, vals, idxs in res]


def _peer_candidates(v1, i1, v2, i2):
    k = v1.shape[0]
    t_ = v1.shape[1]
    brow = lax.broadcasted_iota(jnp.int32, (SUBLANES, t_), 0)
    sums = [v1[0:1] + v2]
    ids = [i1[0:1] * N_KEYS + i2]
    tail = k // 2
    for a in range(1, tail):
        keep = brow < (k // (a + 1))
        sums.append(jnp.where(keep, v1[a:a + 1] + v2[0:SUBLANES], -jnp.inf))
        ids.append(i1[a:a + 1] * N_KEYS + i2[0:SUBLANES])
    sums.append(v1[tail:k] + v2[0:1])
    ids.append(i1[tail:k] * N_KEYS + i2[0:1])
    return jnp.concatenate(sums, axis=0), jnp.concatenate(ids, axis=0)


def _peer_route_kernel(x_ref, g_ref, wq_ref, keys_ref, xn_ref, idx_ref, gate_ref):
    x = x_ref[...]
    ms = jnp.mean(x * x, axis=-1, keepdims=True)
    xn = (x * lax.rsqrt(ms + NORM_EPS)) * g_ref[...]
    xn_ref[...] = xn
    q = jnp.dot(xn.astype(BF16), wq_ref[...], preferred_element_type=F32)
    st = lax.dot_general(keys_ref[...], q.astype(BF16), (((1,), (1,)), ((), ())),
                         preferred_element_type=F32)
    kk = PEER_TOPK
    cands = []
    for h in range(PEER_HEADS):
        base = h * 2 * N_KEYS
        (v1, i1), (v2, i2) = _topk_rows([(st[base:base + N_KEYS], None),
                                         (st[base + N_KEYS:base + 2 * N_KEYS], None)], kk)
        cands.append(_peer_candidates(v1, i1, v2, i2))
    group = 4
    for h0 in range(0, PEER_HEADS, group):
        picked = _topk_rows(cands[h0:h0 + group], kk)
        for h, (sc, idx) in zip(range(h0, h0 + group), picked):
            e = jnp.exp(sc - sc[0:1])
            gate = e / jnp.sum(e, axis=0, keepdims=True)
            idx_ref[h * kk:(h + 1) * kk, :] = idx
            gate_ref[h * kk:(h + 1) * kk, :] = gate


def _peer_route(h, gain, wq, keys_bd, *, tb=128):
    n, d = h.shape
    hk = PEER_HEADS * PEER_TOPK
    return pl.pallas_call(
        _peer_route_kernel,
        grid=(n // tb,),
        in_specs=[
            pl.BlockSpec((tb, d), lambda i: (i, 0)),
            pl.BlockSpec((1, d), lambda i: (0, 0)),
            pl.BlockSpec(wq.shape, lambda i: (0, 0)),
            pl.BlockSpec(keys_bd.shape, lambda i: (0, 0)),
        ],
        out_specs=[
            pl.BlockSpec((tb, d), lambda i: (i, 0)),
            pl.BlockSpec((hk, tb), lambda i: (0, i)),
            pl.BlockSpec((hk, tb), lambda i: (0, i)),
        ],
        out_shape=[
            jax.ShapeDtypeStruct((n, d), F32),
            jax.ShapeDtypeStruct((hk, n), jnp.int32),
            jax.ShapeDtypeStruct((hk, n), F32),
        ],
        compiler_params=_cparams(("parallel",)),
        name="peer_route",
    )(h, gain.reshape(1, d).astype(F32), wq, keys_bd)


def _gelu_tanh(x):
    c = 0.7978845608028654
    return 0.5 * x * (1.0 + jnp.tanh(c * (x + 0.044715 * (x * x * x))))


def _pad_rows(x, rows):
    if x.shape[0] >= rows:
        return x
    return jnp.concatenate([x, jnp.zeros((rows - x.shape[0], x.shape[1]), x.dtype)], axis=0)


def _peer_expert_kernel(idx_ref, gate_ref, xn_ref, h_ref, tab_ref, o_ref, buf_ref, sem_ref,
                        *, d, n_blocks):
    g = pl.program_id(0)
    tb, hk = gate_ref.shape
    pairs = tb * hk
    dr = d // LANES
    rows_per = 2 * dr
    pitch = rows_per + PEER_PITCH_PAD
    slot_rows = pairs * pitch
    slot = g % 2

    @pl.when(g < n_blocks)
    def _():
        base = slot * slot_rows
        for t in range(tb):
            for j in range(hk):
                p = t * hk + j
                pltpu.make_async_copy(tab_ref.at[idx_ref[t, j]],
                                      buf_ref.at[pl.ds(base + p * pitch, rows_per), :],
                                      sem_ref.at[slot]).start(priority=p % 2)

    @pl.when(g >= 1)
    def _():
        cslot = 1 - slot
        cbase = cslot * slot_rows
        whole = buf_ref.at[pl.ds(0, pairs * rows_per), :]
        pltpu.make_async_copy(whole, whole, sem_ref.at[cslot]).wait()

        def rows(r0):
            parts = [buf_ref[pl.ds(cbase + r0 + r, pairs, stride=pitch), :] for r in range(2)]
            return jnp.concatenate(parts, axis=1).astype(BF16)

        x = _pad_rows(xn_ref[...], BF16_ROWS).astype(BF16)
        act = jnp.zeros((BF16_ROWS, pairs), F32)
        for c in range(dr // 2):
            act = act + lax.dot_general(x[:, c * 2 * LANES:(c + 1) * 2 * LANES], rows(2 * c),
                                        (((1,), (1,)), ((), ())), preferred_element_type=F32)
        act = act[:tb]
        gate = jnp.concatenate([gate_ref[...]] * tb, axis=1)
        row = lax.broadcasted_iota(jnp.int32, (tb, pairs), 0)
        col = lax.broadcasted_iota(jnp.int32, (tb, pairs), 1)
        own = (col >= row * hk) & (col < (row + 1) * hk)
        coef = jnp.where(own, gate * _gelu_tanh(act), 0.0)
        coef = _pad_rows(coef, BF16_ROWS).astype(BF16)
        for c in range(dr // 2):
            out = jnp.dot(coef, rows(dr + 2 * c), preferred_element_type=F32)[:tb]
            sl = slice(c * 2 * LANES, (c + 1) * 2 * LANES)
            o_ref[:, sl] = h_ref[:, sl] + out


def _peer_experts(idx, gate, xn, h, table):
    n, d = h.shape
    hk = idx.shape[1]
    tb = PEER_TOKENS_PER_STEP
    n_blocks = n // tb
    rows_per = table.shape[1]
    assert rows_per == 2 * (d // LANES) and (d // LANES) % 2 == 0
    kern = functools.partial(_peer_expert_kernel, d=d, n_blocks=n_blocks)

    def cur(g):
        return (jnp.maximum(g - 1, 0), 0)

    return pl.pallas_call(
        kern,
        grid=(n_blocks + 1,),
        in_specs=[
            pl.BlockSpec((tb, hk), lambda g: (jnp.minimum(g, n_blocks - 1), 0),
                         memory_space=pltpu.SMEM),
            pl.BlockSpec((tb, hk), cur),
            pl.BlockSpec((tb, d), cur),
            pl.BlockSpec((tb, d), cur),
            pl.BlockSpec(memory_space=pl.ANY),
        ],
        out_specs=pl.BlockSpec((tb, d), cur),
        out_shape=jax.ShapeDtypeStruct((n, d), F32),
        scratch_shapes=[pltpu.VMEM((2 * tb * hk * (rows_per + PEER_PITCH_PAD), LANES), F32),
                        pltpu.SemaphoreType.DMA((2,))],
        compiler_params=_cparams(("arbitrary",), vmem=PEER_VMEM_LIMIT),
        name="peer_experts",
    )(idx, gate, xn, h, table)


def _peer_keys_blockdiag(keys1, keys2):
    half = PEER_KEY_DIM // 2
    eye = jnp.eye(PEER_HEADS * 2, dtype=F32)
    keys = jnp.stack([keys1, keys2]).astype(F32)
    keys = jnp.tile(keys, (PEER_HEADS, 1, 1))
    bd = jnp.einsum("gnc,gf->gnfc", keys, eye)
    return bd.reshape(PEER_HEADS * 2 * N_KEYS, PEER_HEADS * 2 * half).astype(BF16)


def _peer_layer(h, gain, wq, keys1, keys2, u, v):
    xn, idx_t, gate_t = _peer_route(h, gain, wq.astype(BF16), _peer_keys_blockdiag(keys1, keys2))
    e, d = u.shape
    table = jnp.concatenate([u.reshape(e, d // LANES, LANES), v.reshape(e, d // LANES, LANES)],
                            axis=1)
    return _peer_experts(idx_t.T, gate_t.T, xn, h, table)


def _rope_tables(positions, head_dim):
    half = head_dim // 2
    inv = ROPE_BASE ** (-jnp.arange(half, dtype=F32) / half)
    ang = positions.astype(F32)[:, None] * inv[None, :]
    return jnp.cos(ang), jnp.sin(ang)


def kernel(x_prompt, x_sample, cache_k, cache_v, state_ret, page_table, norm_mix, norm_ffn,
           sb_wqkv, sb_q_gain, sb_k_gain, sb_bias, sb_wo, ret_wqkvg, ret_norm_gain, ret_wo,
           peer_wq, peer_keys1, peer_keys2, peer_u, peer_v):
    bp, sp, d = x_prompt.shape
    bd, sd, _ = x_sample.shape
    n_p = bp * sp
    n_s = bd * sd
    n_pages = page_table.shape[1]
    past = n_pages * PAGE_SIZE
    dh = d // SB_HEADS
    n_pool = cache_k.shape[1]

    h = jnp.concatenate([x_prompt.reshape(n_p, d), x_sample.reshape(n_s, d)], axis=0)

    wqkv = sb_wqkv[0].astype(BF16)
    qk_gain = jnp.concatenate([jnp.tile(sb_q_gain[0], SB_HEADS),
                               jnp.tile(sb_k_gain[0], SB_HEADS)]).reshape(1, 2 * d).astype(F32)
    qk = _norm_matmul(h, wqkv[:, :2 * d], gain=norm_mix[0], epilogue=_headnorm_epilogue,
                      extras=(qk_gain,),
                      extra_specs=(lambda tm, tn: pl.BlockSpec((1, tn), lambda i, j: (0, j)),),
                      name="sb_qk_proj")
    v = _norm_matmul(h, wqkv[:, 2 * d:], gain=norm_mix[0], name="sb_v_proj")

    bias = sb_bias[0].astype(F32)
    bias_rep = jnp.broadcast_to(bias[:, None, None], (SB_HEADS, 1, LANES))
    att_p = _sb_prompt(qk, v, bias_rep, batch=bp, seq=sp, heads=SB_HEADS)
    bias_rows = jnp.broadcast_to(jnp.repeat(bias, sd)[:, None], (SB_HEADS * sd, LANES))
    att_s = _sb_sample(qk, v,
                       cache_k[0].reshape(n_pool, PAGE_SIZE * SB_HEADS, dh),
                       cache_v[0].reshape(n_pool, PAGE_SIZE * SB_HEADS, dh),
                       page_table, bias_rows, row0=n_p, dec_batch=bd, t=sd, heads=SB_HEADS)
    att = jnp.concatenate([att_p, att_s], axis=0)
    h = _norm_matmul(att, sb_wo[0].astype(BF16), res=h, name="sb_wo")
    h = _peer_layer(h, norm_ffn[0], peer_wq[0], peer_keys1[0], peer_keys2[0], peer_u[0], peer_v[0])

    k_all = qk[:, d:]
    new_k_prompt = k_all[:n_p].reshape(1, bp, sp, SB_HEADS, dh)
    new_k_sample = k_all[n_p:].reshape(1, bd, sd, SB_HEADS, dh)
    new_v_prompt = v[:n_p].reshape(1, bp, sp, SB_HEADS, dh)
    new_v_sample = v[n_p:].reshape(1, bd, sd, SB_HEADS, dh)

    dk = d // RET_HEADS
    dv = 2 * d // RET_HEADS
    hq = RET_HEADS * dk
    hv = RET_HEADS * dv
    w = ret_wqkvg[0].astype(BF16)
    pos = jnp.concatenate([jnp.tile(jnp.arange(sp), bp), jnp.tile(past + jnp.arange(sd), bd)])
    cos, sin = _rope_tables(pos, dk)
    rope = functools.partial(_rope_epilogue, head_dim=dk, q_cols=hq, k_scale=dk ** -0.5)

    def half_spec(tm, tn):
        return pl.BlockSpec((tm, dk // 2), lambda i, j: (i, 0))

    rqk = _norm_matmul(h, w[:, :2 * hq], gain=norm_mix[1], epilogue=rope, extras=(cos, sin),
                       extra_specs=(half_spec, half_spec), name="ret_qk_proj")
    rvg = _norm_matmul(h, w[:, 2 * hq:], gain=norm_mix[1], name="ret_vg_proj")

    log_g = jnp.log1p(-jnp.exp2(-5.0 - jnp.arange(RET_HEADS, dtype=F32)))
    lg_rep = jnp.broadcast_to(log_g[:, None, None], (RET_HEADS, 1, LANES))
    ret_gain = ret_norm_gain[0].reshape(RET_HEADS, 1, dv).astype(F32)
    zero_state = jnp.zeros((bp, RET_HEADS, dk, dv), F32)
    o_p, r_p = _retention(rqk, rvg, zero_state, lg_rep, ret_gain,
                          row0=0, batch=bp, seq=sp, heads=RET_HEADS)
    o_s, r_s = _retention(rqk, rvg, state_ret[0], lg_rep, ret_gain,
                          row0=n_p, batch=bd, seq=sd, heads=RET_HEADS)
    o = jnp.concatenate([o_p, o_s], axis=0)
    h = _norm_matmul(o, ret_wo[0].astype(BF16), res=h, name="ret_wo")
    h = _peer_layer(h, norm_ffn[1], peer_wq[1], peer_keys1[1], peer_keys2[1], peer_u[1], peer_v[1])

    return (h[:n_p].reshape(bp, sp, d), h[n_p:].reshape(bd, sd, d),
            new_k_prompt, new_v_prompt, new_k_sample, new_v_sample,
            r_p[None], r_s[None])
```

```python
import functools

import jax
import jax.numpy as jnp
from jax import lax
from jax.experimental import pallas as pl
from jax.experimental.pallas import tpu as pltpu

F32 = jnp.float32
BF16 = jnp.bfloat16

SB_HEADS = 16
RET_HEADS = 8
PEER_HEADS = 8
PEER_KEY_DIM = 128
N_KEYS = 128
PEER_TOPK = 16
PAGE_SIZE = 128
Q_BLOCK = 128
RET_CHUNK = 128
ROPE_BASE = 10000.0
NORM_EPS = 1e-6

LANES = 128
SUBLANES = 8
BF16_ROWS = 16
VMEM_LIMIT = 52 * 1024 * 1024

PEER_TOKENS_PER_STEP = 8
PEER_PITCH_PAD = 4
PEER_GROUP_PAIRS = (256, 256, 256, 192, 64)
PEER_STARTS_BEFORE_WAIT = (129, 185, 185, 185, 140)
SB_PROMPT_Q_ROWS = 512


def _cparams(sem, vmem=VMEM_LIMIT):
    return pltpu.CompilerParams(dimension_semantics=sem, vmem_limit_bytes=vmem)


def _mm_kernel(*refs, has_gain, has_res, n_extra, epilogue, row_chunk):
    it = iter(refs)
    x_ref = next(it)
    g_ref = next(it) if has_gain else None
    w_ref = next(it)
    r_ref = next(it) if has_res else None
    extra_refs = [next(it) for _ in range(n_extra)]
    o_ref = next(it)
    xn_ref = next(it)
    tm = x_ref.shape[0]

    @pl.when(pl.program_id(1) == 0)
    def _():
        def chunk(c, carry):
            rows = pl.ds(pl.multiple_of(c * row_chunk, row_chunk), row_chunk)
            xv = x_ref[rows, :].astype(F32)
            if has_gain:
                ms = jnp.mean(xv * xv, axis=-1, keepdims=True)
                xv = (xv * lax.rsqrt(ms + NORM_EPS)) * g_ref[...]
            xn_ref[rows, :] = xv.astype(BF16)
            return carry
        lax.fori_loop(0, tm // row_chunk, chunk, 0)

    acc = jnp.dot(xn_ref[...], w_ref[...], preferred_element_type=F32)
    if has_res:
        acc = acc + r_ref[...]
    if epilogue is None:
        o_ref[...] = acc.astype(o_ref.dtype)
    else:
        epilogue(acc, o_ref, extra_refs, pl.program_id(1))


def _norm_matmul(x, w, *, gain=None, res=None, epilogue=None, extras=(), extra_specs=(),
                 tm=512, tn=512, name="mm"):
    n, k = x.shape
    m = w.shape[1]
    tn = min(tn, m)
    assert n % tm == 0 and m % tn == 0, (n, tm, m, tn)
    extra_specs = [make(tm, tn) for make in extra_specs]
    in_specs = [pl.BlockSpec((tm, k), lambda i, j: (i, 0))]
    args = [x]
    if gain is not None:
        in_specs.append(pl.BlockSpec((1, k), lambda i, j: (0, 0)))
        args.append(gain.reshape(1, k).astype(F32))
    in_specs.append(pl.BlockSpec((k, tn), lambda i, j: (0, j)))
    args.append(w)
    if res is not None:
        in_specs.append(pl.BlockSpec((tm, tn), lambda i, j: (i, j)))
        args.append(res)
    in_specs.extend(extra_specs)
    args.extend(extras)
    kern = functools.partial(_mm_kernel, has_gain=gain is not None, has_res=res is not None,
                             n_extra=len(extras), epilogue=epilogue, row_chunk=min(64, tm))
    return pl.pallas_call(
        kern,
        grid=(n // tm, m // tn),
        in_specs=in_specs,
        out_specs=pl.BlockSpec((tm, tn), lambda i, j: (i, j)),
        out_shape=jax.ShapeDtypeStruct((n, m), F32),
        scratch_shapes=[pltpu.VMEM((tm, k), BF16)],
        compiler_params=_cparams(("parallel", "arbitrary")),
        name=name,
    )(*args)


def _headnorm_epilogue(acc, o_ref, extra_refs, j):
    gain_ref, = extra_refs
    for g in range(acc.shape[1] // LANES):
        sl = slice(g * LANES, (g + 1) * LANES)
        y = acc[:, sl]
        ms = jnp.mean(y * y, axis=-1, keepdims=True)
        o_ref[:, sl] = (y * lax.rsqrt(ms + NORM_EPS)) * gain_ref[:, sl]


def _rope_epilogue(acc, o_ref, extra_refs, j, *, head_dim, q_cols, k_scale):
    cos_ref, sin_ref = extra_refs
    cos = cos_ref[...]
    sin = sin_ref[...]
    half = head_dim // 2
    tn = acc.shape[1]
    for g in range(tn // head_dim):
        scale = jnp.where(j * tn + g * head_dim >= q_cols, jnp.float32(k_scale), jnp.float32(1.0))
        x1 = acc[:, g * head_dim:g * head_dim + half]
        x2 = acc[:, g * head_dim + half:(g + 1) * head_dim]
        o_ref[:, g * head_dim:g * head_dim + half] = (x1 * cos - x2 * sin) * scale
        o_ref[:, g * head_dim + half:(g + 1) * head_dim] = (x1 * sin + x2 * cos) * scale


def _softplus(z):
    return jnp.maximum(z, 0.0) + jnp.log1p(jnp.exp(-jnp.abs(z)))


def _suffix_sum_weights(n):
    row = lax.broadcasted_iota(jnp.int32, (n, 2 * n), 0)
    col = lax.broadcasted_iota(jnp.int32, (n, 2 * n), 1)
    return jnp.where((col >= n) | (row > col), 1.0, 0.0).astype(BF16)


def _split_dot(x, w):
    hi = x.astype(BF16)
    lo = (x - hi.astype(F32)).astype(BF16)
    return jnp.dot(jnp.concatenate([hi, lo], axis=1), jnp.concatenate([w, w], axis=0),
                   preferred_element_type=F32)


def _sb_tile(z, valid, carry, lo_w):
    n = z.shape[1]
    sp = _softplus(z)
    log_beta = z - sp
    log_rest = -sp
    if valid is not None:
        log_rest = jnp.where(valid, log_rest, 0.0)
    cs = _split_dot(log_rest, lo_w)
    between = carry + cs[:, :n]
    a = jnp.exp(log_beta + between)
    if valid is not None:
        a = jnp.where(valid, a, 0.0)
    return a, carry + cs[:, n:]


def _sb_prompt_kernel(bias_ref, q_ref, k_ref, v_ref, o_ref, *, scale):
    i = pl.program_id(2)
    tq = q_ref.shape[0]
    tk = LANES
    sub = tq // tk
    q = q_ref[...].astype(BF16)
    bias = bias_ref[...]
    lo_w = _suffix_sum_weights(tk)
    row = lax.broadcasted_iota(jnp.int32, (tq, tk), 0)
    col = lax.broadcasted_iota(jnp.int32, (tq, tk), 1)

    grp = 2 if sub % 2 == 0 else 1

    def tiles(kg, acc, carry, valids):
        rows = pl.ds(pl.multiple_of(kg * (grp * tk), grp * tk), grp * tk)
        k = k_ref[rows, :].astype(BF16)
        v = v_ref[rows, :].astype(BF16)
        z = lax.dot_general(q, k, (((1,), (1,)), ((), ())), preferred_element_type=F32)
        parts = [None] * grp
        for u in reversed(range(grp)):
            zu = z[:, u * tk:(u + 1) * tk] * scale + bias
            parts[u], carry = _sb_tile(zu, valids[u], carry, lo_w)
        a = parts[0] if grp == 1 else jnp.concatenate(parts, axis=1)
        acc = acc + jnp.dot(a.astype(BF16), v, preferred_element_type=F32)
        return acc, carry

    acc = jnp.zeros((tq, v_ref.shape[1]), F32)
    carry = jnp.zeros((tq, tk), F32)
    for dg in reversed(range(sub // grp)):
        valids = [col + (dg * grp + u) * tk < row for u in range(grp)]
        acc, carry = tiles(i * (sub // grp) + dg, acc, carry, valids)

    def body(jj, c):
        return tiles(i * (sub // grp) - 1 - jj, c[0], c[1], [None] * grp)

    acc, carry = lax.fori_loop(0, i * (sub // grp), body, (acc, carry))
    o_ref[...] = acc.astype(o_ref.dtype)


def _sb_prompt(qk, v, bias_rep, *, batch, seq, heads):
    dh = LANES
    d = heads * dh
    tq = SB_PROMPT_Q_ROWS if seq % SB_PROMPT_Q_ROWS == 0 else LANES
    nq = seq // tq
    kern = functools.partial(_sb_prompt_kernel, scale=dh ** -0.5)
    return pl.pallas_call(
        kern,
        grid=(batch, heads, nq),
        in_specs=[
            pl.BlockSpec((None, 1, LANES), lambda b, h, i: (h, 0, 0)),
            pl.BlockSpec((tq, dh), lambda b, h, i: (b * nq + i, h)),
            pl.BlockSpec((seq, dh), lambda b, h, i: (b, heads + h)),
            pl.BlockSpec((seq, dh), lambda b, h, i: (b, h)),
        ],
        out_specs=pl.BlockSpec((tq, dh), lambda b, h, i: (b * nq + i, h)),
        out_shape=jax.ShapeDtypeStruct((batch * seq, d), F32),
        compiler_params=_cparams(("parallel", "parallel", "arbitrary")),
        name="sb_prompt",
    )(bias_rep, qk, qk, v)


def _sb_sample_kernel(pt_ref, bias_ref, q_ref, kn_ref, vn_ref, kp_ref, vp_ref, o_ref,
                      acc_ref, carry_ref, *, heads, scale):
    s = pl.program_id(1)
    t = q_ref.shape[0]
    dh = LANES
    hq = heads * t
    bias = bias_ref[...]
    lo_w = _suffix_sum_weights(LANES)

    def q_head(h):
        qh = q_ref[:, h * dh:(h + 1) * dh]
        pad = jnp.zeros((BF16_ROWS - t, dh), F32)
        return jnp.concatenate([qh, pad], axis=0).astype(BF16)

    def attend(k_of, v_of, valid):
        zs = []
        for h in range(heads):
            zh = lax.dot_general(q_head(h), k_of(h), (((1,), (1,)), ((), ())),
                                 preferred_element_type=F32)
            zs.append(zh[:t])
        z = jnp.concatenate(zs, axis=0) * scale + bias
        a, carry = _sb_tile(z, valid, carry_ref[...], lo_w)
        carry_ref[...] = carry
        for h in range(heads):
            ah = a[h * t:(h + 1) * t]
            ah = jnp.concatenate([ah, jnp.zeros((BF16_ROWS - t, LANES), F32)], axis=0)
            oh = jnp.dot(ah.astype(BF16), v_of(h), preferred_element_type=F32)
            acc_ref[:, h * dh:(h + 1) * dh] += oh[:t]

    @pl.when(s == 0)
    def _():
        acc_ref[...] = jnp.zeros_like(acc_ref)
        carry_ref[...] = jnp.zeros_like(carry_ref)
        row = lax.broadcasted_iota(jnp.int32, (hq, LANES), 0)
        col = lax.broadcasted_iota(jnp.int32, (hq, LANES), 1)
        valid = col < lax.rem(row, t)
        zpad = jnp.zeros((LANES - t, dh), F32)

        def k_of(h):
            return jnp.concatenate([kn_ref[:, h * dh:(h + 1) * dh], zpad], axis=0).astype(BF16)

        def v_of(h):
            return jnp.concatenate([vn_ref[:, h * dh:(h + 1) * dh], zpad], axis=0).astype(BF16)

        attend(k_of, v_of, valid)

    @pl.when(s > 0)
    def _():
        def k_of(h):
            return kp_ref[pl.ds(h, PAGE_SIZE, stride=heads), :].astype(BF16)

        def v_of(h):
            return vp_ref[pl.ds(h, PAGE_SIZE, stride=heads), :].astype(BF16)

        attend(k_of, v_of, None)

    @pl.when(s == pl.num_programs(1) - 1)
    def _():
        o_ref[...] = acc_ref[...]


def _sb_sample(qk, v, cache_k, cache_v, page_table, bias_rows, *, row0, dec_batch, t, heads):
    dh = LANES
    d = heads * dh
    n_pages = page_table.shape[1]
    rb0 = row0 // t

    def page_map(b, s, pt):
        return (pt[b, jnp.clip(n_pages - s, 0, n_pages - 1)], 0, 0)

    kern = functools.partial(_sb_sample_kernel, heads=heads, scale=dh ** -0.5)
    grid_spec = pltpu.PrefetchScalarGridSpec(
        num_scalar_prefetch=1,
        grid=(dec_batch, n_pages + 1),
        in_specs=[
            pl.BlockSpec((heads * t, LANES), lambda b, s, pt: (0, 0)),
            pl.BlockSpec((t, d), lambda b, s, pt: (rb0 + b, 0)),
            pl.BlockSpec((t, d), lambda b, s, pt: (rb0 + b, 1)),
            pl.BlockSpec((t, d), lambda b, s, pt: (rb0 + b, 0)),
            pl.BlockSpec((None, PAGE_SIZE * heads, dh), page_map),
            pl.BlockSpec((None, PAGE_SIZE * heads, dh), page_map),
        ],
        out_specs=pl.BlockSpec((t, d), lambda b, s, pt: (b, 0)),
        scratch_shapes=[pltpu.VMEM((t, d), F32), pltpu.VMEM((heads * t, LANES), F32)],
    )
    return pl.pallas_call(
        kern,
        grid_spec=grid_spec,
        out_shape=jax.ShapeDtypeStruct((dec_batch * t, d), F32),
        compiler_params=_cparams(("parallel", "arbitrary")),
        name="sb_sample",
    )(page_table, bias_rows, qk, qk, v, cache_k, cache_v)


def _retention_kernel(lg_ref, gain_ref, q_ref, k_ref, v_ref, g_ref, s_ref, o_ref, so_ref, r_ref,
                      *, chunk_len):
    c = pl.program_id(2)
    rows_in = q_ref.shape[0]
    cp = LANES
    assert rows_in <= cp
    dk = q_ref.shape[1]
    dv = v_ref.shape[1]

    @pl.when(c == 0)
    def _():
        r_ref[...] = s_ref[...]

    def load(ref):
        x = ref[...]
        if rows_in < cp:
            x = jnp.concatenate([x, jnp.zeros((cp - rows_in, x.shape[1]), F32)], axis=0)
        return x

    lg = lg_ref[...]
    row = lax.broadcasted_iota(jnp.int32, (cp, LANES), 0).astype(F32)
    col = lax.broadcasted_iota(jnp.int32, (cp, LANES), 1).astype(F32)
    decay_query = jnp.exp((row + 1.0) * lg)
    decay_key = jnp.exp((chunk_len - 1.0 - row) * lg)
    decay_chunk = jnp.exp(chunk_len * lg)
    diff = row - col
    decay_intra = jnp.where(diff >= 0, jnp.exp(jnp.maximum(diff, 0.0) * lg), 0.0)

    q = load(q_ref)
    k = load(k_ref)
    v = load(v_ref).astype(BF16)
    qb = q.astype(BF16)
    kb = k.astype(BF16)
    att = lax.dot_general(qb, kb, (((1,), (1,)), ((), ())), preferred_element_type=F32)
    att = att * decay_intra
    inner = jnp.dot(att.astype(BF16), v, preferred_element_type=F32)
    r = r_ref[...]
    cross = jnp.dot(qb, r.astype(BF16), preferred_element_type=F32)
    kd = jnp.concatenate([k[:, s * LANES:(s + 1) * LANES] * decay_key
                          for s in range(dk // LANES)], axis=1).astype(BF16)
    upd = lax.dot_general(kd, v, (((0,), (0,)), ((), ())), preferred_element_type=F32)
    for s in range(dv // LANES):
        sl = slice(s * LANES, (s + 1) * LANES)
        r_ref[:, sl] = decay_chunk * r[:, sl] + upd[:, sl]
    out = jnp.concatenate([inner[:, s * LANES:(s + 1) * LANES]
                           + cross[:, s * LANES:(s + 1) * LANES] * decay_query
                           for s in range(dv // LANES)], axis=1)[:rows_in]
    ms = jnp.mean(out * out, axis=-1, keepdims=True)
    o = (out * lax.rsqrt(ms + NORM_EPS)) * gain_ref[...]
    g = g_ref[...]
    o_ref[...] = (g * jax.nn.sigmoid(g)) * o

    @pl.when(c == pl.num_programs(2) - 1)
    def _():
        so_ref[...] = r_ref[...]


def _retention(qk, vg, state, lg_rep, gain, *, row0, batch, seq, heads):
    dk = qk.shape[1] // (2 * heads)
    dv = vg.shape[1] // (2 * heads)
    assert dk % LANES == 0 and dv % LANES == 0
    c = RET_CHUNK if seq % RET_CHUNK == 0 else seq
    nc = seq // c
    rb0 = row0 // c

    def rows(b, h, ci):
        return rb0 + b * nc + ci

    kern = functools.partial(_retention_kernel, chunk_len=float(c))
    return pl.pallas_call(
        kern,
        grid=(batch, heads, nc),
        in_specs=[
            pl.BlockSpec((None, 1, LANES), lambda b, h, ci: (h, 0, 0)),
            pl.BlockSpec((None, 1, dv), lambda b, h, ci: (h, 0, 0)),
            pl.BlockSpec((c, dk), lambda b, h, ci: (rows(b, h, ci), h)),
            pl.BlockSpec((c, dk), lambda b, h, ci: (rows(b, h, ci), heads + h)),
            pl.BlockSpec((c, dv), lambda b, h, ci: (rows(b, h, ci), h)),
            pl.BlockSpec((c, dv), lambda b, h, ci: (rows(b, h, ci), heads + h)),
            pl.BlockSpec((None, None, dk, dv), lambda b, h, ci: (b, h, 0, 0)),
        ],
        out_specs=[
            pl.BlockSpec((c, dv), lambda b, h, ci: (b * nc + ci, h)),
            pl.BlockSpec((None, None, dk, dv), lambda b, h, ci: (b, h, 0, 0)),
        ],
        out_shape=[
            jax.ShapeDtypeStruct((batch * seq, heads * dv), F32),
            jax.ShapeDtypeStruct((batch, heads, dk, dv), F32),
        ],
        scratch_shapes=[pltpu.VMEM((dk, dv), F32)],
        compiler_params=_cparams(("parallel", "parallel", "arbitrary")),
        name="retention",
    )(lg_rep, gain, qk, qk, vg, vg, state)


def _topk_rows(problems, k):
    t_ = problems[0][0].shape[1]
    krow = lax.broadcasted_iota(jnp.int32, (k, t_), 0)
    rows = [lax.broadcasted_iota(jnp.int32, s.shape, 0) for s, _ in problems]

    def body(r, carry):
        out = []
        for (s, vals, idxs), (_, payload), rw in zip(carry, problems, rows):
            m = jnp.max(s, axis=0, keepdims=True)
            am = jnp.min(jnp.where(s == m, rw, s.shape[0]), axis=0, keepdims=True)
            hit = rw == am
            if payload is None:
                pv = am
            else:
                pv = jnp.max(jnp.where(hit, payload, -1), axis=0, keepdims=True)
            vals = jnp.where(krow == r, m, vals)
            idxs = jnp.where(krow == r, pv, idxs)
            out.append((jnp.where(hit, -jnp.inf, s), vals, idxs))
        return tuple(out)

    init = tuple((s, jnp.zeros((k, t_), F32), jnp.zeros((k, t_), jnp.int32)) for s, _ in problems)
    res = lax.fori_loop(0, k, body, init)
    return [(vals, idxs) for _, vals, idxs in res]


def _peer_candidates(v1, i1, v2, i2):
    k = v1.shape[0]
    t_ = v1.shape[1]
    brow = lax.broadcasted_iota(jnp.int32, (SUBLANES, t_), 0)
    sums = [v1[0:1] + v2]
    ids = [i1[0:1] * N_KEYS + i2]
    tail = k // 2
    for a in range(1, tail):
        keep = brow < (k // (a + 1))
        sums.append(jnp.where(keep, v1[a:a + 1] + v2[0:SUBLANES], -jnp.inf))
        ids.append(i1[a:a + 1] * N_KEYS + i2[0:SUBLANES])
    sums.append(v1[tail:k] + v2[0:1])
    ids.append(i1[tail:k] * N_KEYS + i2[0:1])
    return jnp.concatenate(sums, axis=0), jnp.concatenate(ids, axis=0)


def _peer_route_kernel(x_ref, g_ref, wq_ref, keys_ref, xn_ref, idx_ref, gate_ref):
    x = x_ref[...]
    ms = jnp.mean(x * x, axis=-1, keepdims=True)
    xn = (x * lax.rsqrt(ms + NORM_EPS)) * g_ref[...]
    xn_ref[...] = xn
    q = jnp.dot(xn.astype(BF16), wq_ref[...], preferred_element_type=F32)
    st = lax.dot_general(keys_ref[...], q.astype(BF16), (((1,), (1,)), ((), ())),
                         preferred_element_type=F32)
    kk = PEER_TOPK
    cands = []
    for h in range(PEER_HEADS):
        base = h * 2 * N_KEYS
        (v1, i1), (v2, i2) = _topk_rows([(st[base:base + N_KEYS], None),
                                         (st[base + N_KEYS:base + 2 * N_KEYS], None)], kk)
        cands.append(_peer_candidates(v1, i1, v2, i2))
    group = 4
    for h0 in range(0, PEER_HEADS, group):
        picked = _topk_rows(cands[h0:h0 + group], kk)
        for h, (sc, idx) in zip(range(h0, h0 + group), picked):
            e = jnp.exp(sc - sc[0:1])
            gate = e / jnp.sum(e, axis=0, keepdims=True)
            idx_ref[h * kk:(h + 1) * kk, :] = idx
            gate_ref[h * kk:(h + 1) * kk, :] = gate


def _peer_route(h, gain, wq, keys_bd, *, tb=128):
    n, d = h.shape
    hk = PEER_HEADS * PEER_TOPK
    return pl.pallas_call(
        _peer_route_kernel,
        grid=(n // tb,),
        in_specs=[
            pl.BlockSpec((tb, d), lambda i: (i, 0)),
            pl.BlockSpec((1, d), lambda i: (0, 0)),
            pl.BlockSpec(wq.shape, lambda i: (0, 0)),
            pl.BlockSpec(keys_bd.shape, lambda i: (0, 0)),
        ],
        out_specs=[
            pl.BlockSpec((tb, d), lambda i: (i, 0)),
            pl.BlockSpec((hk, tb), lambda i: (0, i)),
            pl.BlockSpec((hk, tb), lambda i: (0, i)),
        ],
        out_shape=[
            jax.ShapeDtypeStruct((n, d), F32),
            jax.ShapeDtypeStruct((hk, n), jnp.int32),
            jax.ShapeDtypeStruct((hk, n), F32),
        ],
        compiler_params=_cparams(("parallel",)),
        name="peer_route",
    )(h, gain.reshape(1, d).astype(F32), wq, keys_bd)


def _gelu_tanh(x):
    c = 0.7978845608028654
    return 0.5 * x * (1.0 + jnp.tanh(c * (x + 0.044715 * (x * x * x))))


def _peer_expert_kernel(idx_ref, gate_ref, xn_ref, h_ref, tab_ref, o_ref, buf0_ref, buf1_ref,
                        sem_ref, *, d, n_blocks):
    g = pl.program_id(0)
    tb, hk = idx_ref.shape
    pairs = tb * hk
    half = d // 2
    wr = half // LANES
    rows_per = 2 * wr
    pitch = rows_per + PEER_PITCH_PAD
    cw = 2 if wr % 2 == 0 else 1
    bufs = (buf0_ref, buf1_ref)

    assert sum(PEER_GROUP_PAIRS) == pairs and len(PEER_STARTS_BEFORE_WAIT) == len(PEER_GROUP_PAIRS)
    n_grp = len(PEER_GROUP_PAIRS)
    g_lo = [sum(PEER_GROUP_PAIRS[:k]) for k in range(n_grp)]
    group_of = [k for k, n in enumerate(PEER_GROUP_PAIRS) for _ in range(n)]
    kc = cw * LANES
    n_chunks = wr // cw

    def issue(slot, p0, p1):
        for p in range(p0, p1):
            pltpu.make_async_copy(tab_ref.at[idx_ref[p // hk, p % hk]],
                                  bufs[slot].at[pl.ds(p * pitch, rows_per), :],
                                  sem_ref.at[slot, group_of[p]]).start(priority=p % 2)

    def wait(slot, grp):
        part = bufs[slot].at[pl.ds(0, PEER_GROUP_PAIRS[grp] * rows_per), :]
        pltpu.make_async_copy(part, part, sem_ref.at[slot, grp]).wait()

    def words(slot, r0, p0, n):
        parts = [pltpu.bitcast(bufs[slot][pl.ds(p0 * pitch + r0 + r, n, stride=pitch), :], BF16)
                 for r in range(cw)]
        return parts[0] if cw == 1 else jnp.concatenate(parts, axis=1)

    def split_x():
        x = xn_ref[...]
        return jnp.concatenate([x[:, :half], x[:, half:]], axis=0).astype(BF16)

    def first_stage(slot, grp, x2):
        gp = PEER_GROUP_PAIRS[grp]
        r = jnp.zeros((2 * tb, 2 * gp), F32)
        for c in range(n_chunks):
            r = r + lax.dot_general(x2[:, c * kc:(c + 1) * kc], words(slot, c * cw, g_lo[grp], gp),
                                    (((1,), (1,)), ((), ())), preferred_element_type=F32)
        return r

    def second_stage(slot, rs, before_chunk=lambda c: None):
        r = rs[0] if n_grp == 1 else jnp.concatenate(rs, axis=1)
        act = r[:tb] + pltpu.roll(r[tb:], 2 * pairs - 1, axis=1)
        gate = jnp.concatenate([gate_ref[...]] * tb, axis=1)
        row = lax.broadcasted_iota(jnp.int32, (tb, 2 * pairs), 0)
        lane = lax.broadcasted_iota(jnp.int32, (tb, 2 * pairs), 1)
        own = ((lane >= row * (2 * hk)) & (lane < (row + 1) * (2 * hk))
               & (jnp.bitwise_and(lane, 1) == 0))
        coef_lo = jnp.where(own, gate * _gelu_tanh(act), 0.0)
        coef_hi = pltpu.roll(coef_lo, 1, axis=1)
        coef = jnp.concatenate([coef_lo, coef_hi], axis=0).astype(BF16)
        for c in range(n_chunks):
            before_chunk(c)
            out = jnp.dot(coef, words(slot, wr + c * cw, 0, pairs),
                          preferred_element_type=F32)
            lo = slice(c * kc, (c + 1) * kc)
            hi = slice(half + c * kc, half + (c + 1) * kc)
            o_ref[:, lo] = h_ref[:, lo] + out[:tb]
            o_ref[:, hi] = h_ref[:, hi] + out[tb:]

    @pl.when(g == 0)
    def _():
        issue(0, 0, pairs)

    cuts = [min(pairs, sum(PEER_STARTS_BEFORE_WAIT[:k])) for k in range(n_grp + 1)] + [pairs]
    for par in range(2):
        @pl.when((g >= 1) & (g < n_blocks) & (g % 2 == par))
        def _():
            x2 = split_x()
            rs = []
            for grp in range(n_grp):
                issue(par, cuts[grp], cuts[grp + 1])
                wait(1 - par, grp)
                rs.append(first_stage(1 - par, grp, x2))
            late0 = cuts[n_grp]
            per = -(-(pairs - late0) // n_chunks)
            second_stage(1 - par, rs,
                         lambda c: issue(par, min(pairs, late0 + c * per),
                                         min(pairs, late0 + (c + 1) * per)))

    @pl.when(g == n_blocks)
    def _():
        last = (n_blocks - 1) % 2
        x2 = split_x()
        rs = []
        for grp in range(n_grp):
            wait(last, grp)
            rs.append(first_stage(last, grp, x2))
        second_stage(last, rs)


def _peer_experts(idx, gate2, xn, h, table):
    n, d = h.shape
    hk = idx.shape[1]
    tb = PEER_TOKENS_PER_STEP
    n_blocks = n // tb
    rows_per = table.shape[1]
    assert rows_per == d // LANES and n % tb == 0
    kern = functools.partial(_peer_expert_kernel, d=d, n_blocks=n_blocks)

    def cur(g):
        return (jnp.maximum(g - 1, 0), 0)

    slot = pltpu.VMEM((tb * hk * (rows_per + PEER_PITCH_PAD), LANES), jnp.uint32)
    return pl.pallas_call(
        kern,
        grid=(n_blocks + 1,),
        in_specs=[
            pl.BlockSpec((tb, hk), lambda g: (jnp.minimum(g, n_blocks - 1), 0),
                         memory_space=pltpu.SMEM),
            pl.BlockSpec((tb, 2 * hk), cur),
            pl.BlockSpec((tb, d), cur),
            pl.BlockSpec((tb, d), cur),
            pl.BlockSpec(memory_space=pl.ANY),
        ],
        out_specs=pl.BlockSpec((tb, d), cur),
        out_shape=jax.ShapeDtypeStruct((n, d), F32),
        scratch_shapes=[slot, slot, pltpu.SemaphoreType.DMA((2, len(PEER_GROUP_PAIRS)))],
        compiler_params=_cparams(("arbitrary",)),
        name="peer_experts",
    )(idx, gate2, xn, h, table)


def _pack_bf16_halves(w):
    e, d = w.shape
    bits = lax.bitcast_convert_type(w.astype(BF16), jnp.uint16).astype(jnp.uint32)
    words = bits[:, :d // 2] | (bits[:, d // 2:] << 16)
    return words.reshape(e, d // 2 // LANES, LANES)


def _peer_keys_blockdiag(keys1, keys2):
    half = PEER_KEY_DIM // 2
    eye = jnp.eye(PEER_HEADS * 2, dtype=F32)
    keys = jnp.stack([keys1, keys2]).astype(F32)
    keys = jnp.tile(keys, (PEER_HEADS, 1, 1))
    bd = jnp.einsum("gnc,gf->gnfc", keys, eye)
    return bd.reshape(PEER_HEADS * 2 * N_KEYS, PEER_HEADS * 2 * half).astype(BF16)


def _peer_layer(h, gain, wq, keys1, keys2, u, v):
    xn, idx_t, gate_t = _peer_route(h, gain, wq.astype(BF16), _peer_keys_blockdiag(keys1, keys2))
    table = jnp.concatenate([_pack_bf16_halves(u), _pack_bf16_halves(v)], axis=1)
    gate2 = jnp.repeat(gate_t.T, 2, axis=1)
    return _peer_experts(idx_t.T, gate2, xn, h, table)


def _rope_tables(positions, head_dim):
    half = head_dim // 2
    inv = ROPE_BASE ** (-jnp.arange(half, dtype=F32) / half)
    ang = positions.astype(F32)[:, None] * inv[None, :]
    return jnp.cos(ang), jnp.sin(ang)


def kernel(x_prompt, x_sample, cache_k, cache_v, state_ret, page_table, norm_mix, norm_ffn,
           sb_wqkv, sb_q_gain, sb_k_gain, sb_bias, sb_wo, ret_wqkvg, ret_norm_gain, ret_wo,
           peer_wq, peer_keys1, peer_keys2, peer_u, peer_v):
    bp, sp, d = x_prompt.shape
    bd, sd, _ = x_sample.shape
    n_p = bp * sp
    n_s = bd * sd
    n_pages = page_table.shape[1]
    past = n_pages * PAGE_SIZE
    dh = d // SB_HEADS
    n_pool = cache_k.shape[1]

    h = jnp.concatenate([x_prompt.reshape(n_p, d), x_sample.reshape(n_s, d)], axis=0)

    wqkv = sb_wqkv[0].astype(BF16)
    qk_gain = jnp.concatenate([jnp.tile(sb_q_gain[0], SB_HEADS),
                               jnp.tile(sb_k_gain[0], SB_HEADS)]).reshape(1, 2 * d).astype(F32)
    qk = _norm_matmul(h, wqkv[:, :2 * d], gain=norm_mix[0], epilogue=_headnorm_epilogue,
                      extras=(qk_gain,),
                      extra_specs=(lambda tm, tn: pl.BlockSpec((1, tn), lambda i, j: (0, j)),),
                      name="sb_qk_proj")
    v = _norm_matmul(h, wqkv[:, 2 * d:], gain=norm_mix[0], name="sb_v_proj")

    bias = sb_bias[0].astype(F32)
    bias_rep = jnp.broadcast_to(bias[:, None, None], (SB_HEADS, 1, LANES))
    att_p = _sb_prompt(qk, v, bias_rep, batch=bp, seq=sp, heads=SB_HEADS)
    bias_rows = jnp.broadcast_to(jnp.repeat(bias, sd)[:, None], (SB_HEADS * sd, LANES))
    att_s = _sb_sample(qk, v,
                       cache_k[0].reshape(n_pool, PAGE_SIZE * SB_HEADS, dh),
                       cache_v[0].reshape(n_pool, PAGE_SIZE * SB_HEADS, dh),
                       page_table, bias_rows, row0=n_p, dec_batch=bd, t=sd, heads=SB_HEADS)
    att = jnp.concatenate([att_p, att_s], axis=0)
    h = _norm_matmul(att, sb_wo[0].astype(BF16), res=h, name="sb_wo")
    h = _peer_layer(h, norm_ffn[0], peer_wq[0], peer_keys1[0], peer_keys2[0], peer_u[0], peer_v[0])

    k_all = qk[:, d:]
    new_k_prompt = k_all[:n_p].reshape(1, bp, sp, SB_HEADS, dh)
    new_k_sample = k_all[n_p:].reshape(1, bd, sd, SB_HEADS, dh)
    new_v_prompt = v[:n_p].reshape(1, bp, sp, SB_HEADS, dh)
    new_v_sample = v[n_p:].reshape(1, bd, sd, SB_HEADS, dh)

    dk = d // RET_HEADS
    dv = 2 * d // RET_HEADS
    hq = RET_HEADS * dk
    hv = RET_HEADS * dv
    w = ret_wqkvg[0].astype(BF16)
    pos = jnp.concatenate([jnp.tile(jnp.arange(sp), bp), jnp.tile(past + jnp.arange(sd), bd)])
    cos, sin = _rope_tables(pos, dk)
    rope = functools.partial(_rope_epilogue, head_dim=dk, q_cols=hq, k_scale=dk ** -0.5)

    def half_spec(tm, tn):
        return pl.BlockSpec((tm, dk // 2), lambda i, j: (i, 0))

    rqk = _norm_matmul(h, w[:, :2 * hq], gain=norm_mix[1], epilogue=rope, extras=(cos, sin),
                       extra_specs=(half_spec, half_spec), name="ret_qk_proj")
    rvg = _norm_matmul(h, w[:, 2 * hq:], gain=norm_mix[1], name="ret_vg_proj")

    log_g = jnp.log1p(-jnp.exp2(-5.0 - jnp.arange(RET_HEADS, dtype=F32)))
    lg_rep = jnp.broadcast_to(log_g[:, None, None], (RET_HEADS, 1, LANES))
    ret_gain = ret_norm_gain[0].reshape(RET_HEADS, 1, dv).astype(F32)
    zero_state = jnp.zeros((bp, RET_HEADS, dk, dv), F32)
    o_p, r_p = _retention(rqk, rvg, zero_state, lg_rep, ret_gain,
                          row0=0, batch=bp, seq=sp, heads=RET_HEADS)
    o_s, r_s = _retention(rqk, rvg, state_ret[0], lg_rep, ret_gain,
                          row0=n_p, batch=bd, seq=sd, heads=RET_HEADS)
    o = jnp.concatenate([o_p, o_s], axis=0)
    h = _norm_matmul(o, ret_wo[0].astype(BF16), res=h, name="ret_wo")
    h = _peer_layer(h, norm_ffn[1], peer_wq[1], peer_keys1[1], peer_keys2[1], peer_u[1], peer_v[1])

    return (h[:n_p].reshape(bp, sp, d), h[n_p:].reshape(bd, sd, d),
            new_k_prompt, new_v_prompt, new_k_sample, new_v_sample,
            r_p[None], r_s[None])
```

```python
import functools

import jax
import jax.numpy as jnp
from jax import lax
from jax.experimental import pallas as pl
from jax.experimental.pallas import tpu as pltpu

F32 = jnp.float32
BF16 = jnp.bfloat16

SB_HEADS = 16
RET_HEADS = 8
PEER_HEADS = 8
PEER_KEY_DIM = 128
N_KEYS = 128
PEER_TOPK = 16
PAGE_SIZE = 128
Q_BLOCK = 128
RET_CHUNK = 128
ROPE_BASE = 10000.0
NORM_EPS = 1e-6

LANES = 128
SUBLANES = 8
BF16_ROWS = 16
VMEM_LIMIT = 52 * 1024 * 1024

PEER_TOKENS_PER_STEP = 8
PEER_PITCH_PAD = 4
PEER_GROUP_PAIRS = (256, 256, 256, 192, 64)
PEER_STARTS_BEFORE_WAIT = (0, 150, 150, 150, 276)
SB_PROMPT_Q_ROWS = 512
SB_HEAD_PITCH_PAD = 8


def _cparams(sem, vmem=VMEM_LIMIT):
    return pltpu.CompilerParams(dimension_semantics=sem, vmem_limit_bytes=vmem)


def _mm_kernel(*refs, has_gain, has_res, n_extra, epilogue, row_chunk):
    it = iter(refs)
    x_ref = next(it)
    g_ref = next(it) if has_gain else None
    w_ref = next(it)
    r_ref = next(it) if has_res else None
    extra_refs = [next(it) for _ in range(n_extra)]
    o_ref = next(it)
    xn_ref = next(it)
    tm = x_ref.shape[0]

    @pl.when(pl.program_id(1) == 0)
    def _():
        def chunk(c, carry):
            rows = pl.ds(pl.multiple_of(c * row_chunk, row_chunk), row_chunk)
            xv = x_ref[rows, :].astype(F32)
            if has_gain:
                ms = jnp.mean(xv * xv, axis=-1, keepdims=True)
                xv = (xv * lax.rsqrt(ms + NORM_EPS)) * g_ref[...]
            xn_ref[rows, :] = xv.astype(BF16)
            return carry
        lax.fori_loop(0, tm // row_chunk, chunk, 0)

    acc = jnp.dot(xn_ref[...], w_ref[...], preferred_element_type=F32)
    if has_res:
        acc = acc + r_ref[...]
    if epilogue is None:
        o_ref[...] = acc.astype(o_ref.dtype)
    else:
        epilogue(acc, o_ref, extra_refs, pl.program_id(1))


def _norm_matmul(x, w, *, gain=None, res=None, epilogue=None, extras=(), extra_specs=(),
                 tm=512, tn=512, name="mm"):
    n, k = x.shape
    m = w.shape[1]
    tn = min(tn, m)
    assert n % tm == 0 and m % tn == 0, (n, tm, m, tn)
    extra_specs = [make(tm, tn) for make in extra_specs]
    in_specs = [pl.BlockSpec((tm, k), lambda i, j: (i, 0))]
    args = [x]
    if gain is not None:
        in_specs.append(pl.BlockSpec((1, k), lambda i, j: (0, 0)))
        args.append(gain.reshape(1, k).astype(F32))
    in_specs.append(pl.BlockSpec((k, tn), lambda i, j: (0, j)))
    args.append(w)
    if res is not None:
        in_specs.append(pl.BlockSpec((tm, tn), lambda i, j: (i, j)))
        args.append(res)
    in_specs.extend(extra_specs)
    args.extend(extras)
    kern = functools.partial(_mm_kernel, has_gain=gain is not None, has_res=res is not None,
                             n_extra=len(extras), epilogue=epilogue, row_chunk=min(64, tm))
    return pl.pallas_call(
        kern,
        grid=(n // tm, m // tn),
        in_specs=in_specs,
        out_specs=pl.BlockSpec((tm, tn), lambda i, j: (i, j)),
        out_shape=jax.ShapeDtypeStruct((n, m), F32),
        scratch_shapes=[pltpu.VMEM((tm, k), BF16)],
        compiler_params=_cparams(("parallel", "arbitrary")),
        name=name,
    )(*args)


def _headnorm_epilogue(acc, o_ref, extra_refs, j):
    gain_ref, = extra_refs
    for g in range(acc.shape[1] // LANES):
        sl = slice(g * LANES, (g + 1) * LANES)
        y = acc[:, sl]
        ms = jnp.mean(y * y, axis=-1, keepdims=True)
        o_ref[:, sl] = (y * lax.rsqrt(ms + NORM_EPS)) * gain_ref[:, sl]


def _rope_epilogue(acc, o_ref, extra_refs, j, *, head_dim, q_cols, k_scale):
    cos_ref, sin_ref = extra_refs
    cos = cos_ref[...]
    sin = sin_ref[...]
    half = head_dim // 2
    tn = acc.shape[1]
    for g in range(tn // head_dim):
        scale = jnp.where(j * tn + g * head_dim >= q_cols, jnp.float32(k_scale), jnp.float32(1.0))
        x1 = acc[:, g * head_dim:g * head_dim + half]
        x2 = acc[:, g * head_dim + half:(g + 1) * head_dim]
        o_ref[:, g * head_dim:g * head_dim + half] = (x1 * cos - x2 * sin) * scale
        o_ref[:, g * head_dim + half:(g + 1) * head_dim] = (x1 * sin + x2 * cos) * scale


def _softplus(z):
    return jnp.maximum(z, 0.0) + jnp.log1p(jnp.exp(-jnp.abs(z)))


def _suffix_sum_weights(n):
    row = lax.broadcasted_iota(jnp.int32, (n, 2 * n), 0)
    col = lax.broadcasted_iota(jnp.int32, (n, 2 * n), 1)
    return jnp.where((col >= n) | (row > col), 1.0, 0.0).astype(BF16)


def _split_dot(x, w):
    hi = x.astype(BF16)
    lo = (x - hi.astype(F32)).astype(BF16)
    return jnp.dot(jnp.concatenate([hi, lo], axis=1), jnp.concatenate([w, w], axis=0),
                   preferred_element_type=F32)


def _sb_tile(z, valid, carry, lo_w):
    n = z.shape[1]
    sp = _softplus(z)
    log_beta = z - sp
    log_rest = -sp
    if valid is not None:
        log_rest = jnp.where(valid, log_rest, 0.0)
    cs = _split_dot(log_rest, lo_w)
    between = carry + cs[:, :n]
    a = jnp.exp(log_beta + between)
    if valid is not None:
        a = jnp.where(valid, a, 0.0)
    return a, carry + cs[:, n:]


def _sb_prompt_kernel(bias_ref, q_ref, k_ref, v_ref, o_ref, *, scale):
    i = pl.program_id(2)
    tq = q_ref.shape[0]
    tk = LANES
    sub = tq // tk
    q = q_ref[...].astype(BF16)
    bias = bias_ref[...]
    lo_w = _suffix_sum_weights(tk)
    row = lax.broadcasted_iota(jnp.int32, (tq, tk), 0)
    col = lax.broadcasted_iota(jnp.int32, (tq, tk), 1)

    grp = next(n for n in (4, 2, 1) if sub % n == 0)

    def tiles(kg, acc, carry, valids):
        rows = pl.ds(pl.multiple_of(kg * (grp * tk), grp * tk), grp * tk)
        k = k_ref[rows, :].astype(BF16)
        v = v_ref[rows, :].astype(BF16)
        z = lax.dot_general(q, k, (((1,), (1,)), ((), ())), preferred_element_type=F32)
        parts = [None] * grp
        for u in reversed(range(grp)):
            zu = z[:, u * tk:(u + 1) * tk] * scale + bias
            parts[u], carry = _sb_tile(zu, valids[u], carry, lo_w)
        a = parts[0] if grp == 1 else jnp.concatenate(parts, axis=1)
        acc = acc + jnp.dot(a.astype(BF16), v, preferred_element_type=F32)
        return acc, carry

    acc = jnp.zeros((tq, v_ref.shape[1]), F32)
    carry = jnp.zeros((tq, tk), F32)
    for dg in reversed(range(sub // grp)):
        valids = [col + (dg * grp + u) * tk < row for u in range(grp)]
        acc, carry = tiles(i * (sub // grp) + dg, acc, carry, valids)

    def body(jj, c):
        return tiles(i * (sub // grp) - 1 - jj, c[0], c[1], [None] * grp)

    acc, carry = lax.fori_loop(0, i * (sub // grp), body, (acc, carry))
    o_ref[...] = acc.astype(o_ref.dtype)


def _sb_prompt(qk, v, bias_rep, *, batch, seq, heads):
    dh = LANES
    d = heads * dh
    tq = SB_PROMPT_Q_ROWS if seq % SB_PROMPT_Q_ROWS == 0 else LANES
    nq = seq // tq
    kern = functools.partial(_sb_prompt_kernel, scale=dh ** -0.5)
    return pl.pallas_call(
        kern,
        grid=(batch, heads, nq),
        in_specs=[
            pl.BlockSpec((None, 1, LANES), lambda b, h, i: (h, 0, 0)),
            pl.BlockSpec((tq, dh), lambda b, h, i: (b * nq + i, h)),
            pl.BlockSpec((seq, dh), lambda b, h, i: (b, heads + h)),
            pl.BlockSpec((seq, dh), lambda b, h, i: (b, h)),
        ],
        out_specs=pl.BlockSpec((tq, dh), lambda b, h, i: (b * nq + i, h)),
        out_shape=jax.ShapeDtypeStruct((batch * seq, d), F32),
        compiler_params=_cparams(("parallel", "parallel", "arbitrary")),
        name="sb_prompt",
    )(bias_rep, qk, qk, v)


def _sb_sample_kernel(pt_ref, bias_ref, q_ref, kn_ref, vn_ref, ck_ref, cv_ref, o_ref,
                      acc_ref, carry_ref, kbuf_ref, vbuf_ref, sem_ref, *, heads, n_pages, scale):
    b = pl.program_id(0)
    s = pl.program_id(1)
    t = q_ref.shape[0]
    pitch = heads + SB_HEAD_PITCH_PAD

    def page_copy(cache_ref, buf_ref, page, slot, which):
        return pltpu.make_async_copy(
            cache_ref.at[page],
            buf_ref.at[pl.ds(slot * PAGE_SIZE, PAGE_SIZE), pl.ds(0, heads), :],
            sem_ref.at[which, slot])

    def start_page(k, seq, j):
        page = pt_ref[seq, n_pages - 1 - j]
        page_copy(ck_ref, kbuf_ref, page, k % 2, 0).start()
        page_copy(cv_ref, vbuf_ref, page, k % 2, 1).start()
    dh = LANES
    hq = heads * t
    bias = bias_ref[...]
    lo_w = _suffix_sum_weights(LANES)

    def q_head(h):
        qh = q_ref[:, h * dh:(h + 1) * dh]
        pad = jnp.zeros((BF16_ROWS - t, dh), F32)
        return jnp.concatenate([qh, pad], axis=0).astype(BF16)

    def attend(k_of, v_of, valid):
        zs = []
        for h in range(heads):
            zh = lax.dot_general(q_head(h), k_of(h), (((1,), (1,)), ((), ())),
                                 preferred_element_type=F32)
            zs.append(zh[:t])
        z = jnp.concatenate(zs, axis=0) * scale + bias
        a, carry = _sb_tile(z, valid, carry_ref[...], lo_w)
        carry_ref[...] = carry
        for h in range(heads):
            ah = a[h * t:(h + 1) * t]
            ah = jnp.concatenate([ah, jnp.zeros((BF16_ROWS - t, LANES), F32)], axis=0)
            oh = jnp.dot(ah.astype(BF16), v_of(h), preferred_element_type=F32)
            acc_ref[:, h * dh:(h + 1) * dh] += oh[:t]

    @pl.when((s == 0) & (b == 0))
    def _():
        start_page(0, 0, 0)

    @pl.when(s == 0)
    def _():
        acc_ref[...] = jnp.zeros_like(acc_ref)
        carry_ref[...] = jnp.zeros_like(carry_ref)
        row = lax.broadcasted_iota(jnp.int32, (hq, LANES), 0)
        col = lax.broadcasted_iota(jnp.int32, (hq, LANES), 1)
        valid = col < lax.rem(row, t)
        zpad = jnp.zeros((LANES - t, dh), F32)

        def k_of(h):
            return jnp.concatenate([kn_ref[:, h * dh:(h + 1) * dh], zpad], axis=0).astype(BF16)

        def v_of(h):
            return jnp.concatenate([vn_ref[:, h * dh:(h + 1) * dh], zpad], axis=0).astype(BF16)

        attend(k_of, v_of, valid)

    @pl.when(s > 0)
    def _():
        k = b * n_pages + s - 1
        slot = k % 2

        @pl.when(s < n_pages)
        def _():
            start_page(k + 1, b, s)

        @pl.when((s == n_pages) & (b + 1 < pl.num_programs(0)))
        def _():
            start_page(k + 1, b + 1, 0)

        page_copy(ck_ref, kbuf_ref, 0, slot, 0).wait()
        page_copy(cv_ref, vbuf_ref, 0, slot, 1).wait()
        flat = (2 * PAGE_SIZE * pitch, LANES)

        def k_of(h):
            return kbuf_ref.reshape(*flat)[pl.ds(slot * PAGE_SIZE * pitch + h, PAGE_SIZE,
                                                 stride=pitch), :].astype(BF16)

        def v_of(h):
            return vbuf_ref.reshape(*flat)[pl.ds(slot * PAGE_SIZE * pitch + h, PAGE_SIZE,
                                                 stride=pitch), :].astype(BF16)

        attend(k_of, v_of, None)

    @pl.when(s == pl.num_programs(1) - 1)
    def _():
        o_ref[...] = acc_ref[...]


def _sb_sample(qk, v, cache_k, cache_v, page_table, bias_rows, *, row0, dec_batch, t, heads):
    dh = LANES
    d = heads * dh
    n_pages = page_table.shape[1]
    rb0 = row0 // t
    ring = pltpu.VMEM((2 * PAGE_SIZE, heads + SB_HEAD_PITCH_PAD, dh), F32)

    kern = functools.partial(_sb_sample_kernel, heads=heads, n_pages=n_pages, scale=dh ** -0.5)
    grid_spec = pltpu.PrefetchScalarGridSpec(
        num_scalar_prefetch=1,
        grid=(dec_batch, n_pages + 1),
        in_specs=[
            pl.BlockSpec((heads * t, LANES), lambda b, s, pt: (0, 0)),
            pl.BlockSpec((t, d), lambda b, s, pt: (rb0 + b, 0)),
            pl.BlockSpec((t, d), lambda b, s, pt: (rb0 + b, 1)),
            pl.BlockSpec((t, d), lambda b, s, pt: (rb0 + b, 0)),
            pl.BlockSpec(memory_space=pl.ANY),
            pl.BlockSpec(memory_space=pl.ANY),
        ],
        out_specs=pl.BlockSpec((t, d), lambda b, s, pt: (b, 0)),
        scratch_shapes=[pltpu.VMEM((t, d), F32), pltpu.VMEM((heads * t, LANES), F32),
                        ring, ring, pltpu.SemaphoreType.DMA((2, 2))],
    )
    return pl.pallas_call(
        kern,
        grid_spec=grid_spec,
        out_shape=jax.ShapeDtypeStruct((dec_batch * t, d), F32),
        compiler_params=_cparams(("arbitrary", "arbitrary")),
        name="sb_sample",
    )(page_table, bias_rows, qk, qk, v, cache_k, cache_v)


def _retention_kernel(lg_ref, gain_ref, q_ref, k_ref, v_ref, g_ref, s_ref, o_ref, so_ref, r_ref,
                      *, chunk_len):
    c = pl.program_id(2)
    rows_in = q_ref.shape[0]
    cp = LANES
    assert rows_in <= cp
    dk = q_ref.shape[1]
    dv = v_ref.shape[1]

    @pl.when(c == 0)
    def _():
        r_ref[...] = s_ref[...]

    def load(ref):
        x = ref[...]
        if rows_in < cp:
            x = jnp.concatenate([x, jnp.zeros((cp - rows_in, x.shape[1]), F32)], axis=0)
        return x

    lg = lg_ref[...]
    row = lax.broadcasted_iota(jnp.int32, (cp, LANES), 0).astype(F32)
    col = lax.broadcasted_iota(jnp.int32, (cp, LANES), 1).astype(F32)
    decay_query = jnp.exp((row + 1.0) * lg)
    decay_key = jnp.exp((chunk_len - 1.0 - row) * lg)
    decay_chunk = jnp.exp(chunk_len * lg)
    diff = row - col
    decay_intra = jnp.where(diff >= 0, jnp.exp(jnp.maximum(diff, 0.0) * lg), 0.0)

    q = load(q_ref)
    k = load(k_ref)
    v = load(v_ref).astype(BF16)
    qb = q.astype(BF16)
    kb = k.astype(BF16)
    att = lax.dot_general(qb, kb, (((1,), (1,)), ((), ())), preferred_element_type=F32)
    att = att * decay_intra
    inner = jnp.dot(att.astype(BF16), v, preferred_element_type=F32)
    r = r_ref[...]
    cross = jnp.dot(qb, r.astype(BF16), preferred_element_type=F32)
    kd = jnp.concatenate([k[:, s * LANES:(s + 1) * LANES] * decay_key
                          for s in range(dk // LANES)], axis=1).astype(BF16)
    upd = lax.dot_general(kd, v, (((0,), (0,)), ((), ())), preferred_element_type=F32)
    for s in range(dv // LANES):
        sl = slice(s * LANES, (s + 1) * LANES)
        r_ref[:, sl] = decay_chunk * r[:, sl] + upd[:, sl]
    out = jnp.concatenate([inner[:, s * LANES:(s + 1) * LANES]
                           + cross[:, s * LANES:(s + 1) * LANES] * decay_query
                           for s in range(dv // LANES)], axis=1)[:rows_in]
    ms = jnp.mean(out * out, axis=-1, keepdims=True)
    o = (out * lax.rsqrt(ms + NORM_EPS)) * gain_ref[...]
    g = g_ref[...]
    o_ref[...] = (g * jax.nn.sigmoid(g)) * o

    @pl.when(c == pl.num_programs(2) - 1)
    def _():
        so_ref[...] = r_ref[...]


def _retention(qk, vg, state, lg_rep, gain, *, row0, batch, seq, heads):
    dk = qk.shape[1] // (2 * heads)
    dv = vg.shape[1] // (2 * heads)
    assert dk % LANES == 0 and dv % LANES == 0
    c = RET_CHUNK if seq % RET_CHUNK == 0 else seq
    nc = seq // c
    rb0 = row0 // c

    def rows(b, h, ci):
        return rb0 + b * nc + ci

    kern = functools.partial(_retention_kernel, chunk_len=float(c))
    return pl.pallas_call(
        kern,
        grid=(batch, heads, nc),
        in_specs=[
            pl.BlockSpec((None, 1, LANES), lambda b, h, ci: (h, 0, 0)),
            pl.BlockSpec((None, 1, dv), lambda b, h, ci: (h, 0, 0)),
            pl.BlockSpec((c, dk), lambda b, h, ci: (rows(b, h, ci), h)),
            pl.BlockSpec((c, dk), lambda b, h, ci: (rows(b, h, ci), heads + h)),
            pl.BlockSpec((c, dv), lambda b, h, ci: (rows(b, h, ci), h)),
            pl.BlockSpec((c, dv), lambda b, h, ci: (rows(b, h, ci), heads + h)),
            pl.BlockSpec((None, None, dk, dv), lambda b, h, ci: (b, h, 0, 0)),
        ],
        out_specs=[
            pl.BlockSpec((c, dv), lambda b, h, ci: (b * nc + ci, h)),
            pl.BlockSpec((None, None, dk, dv), lambda b, h, ci: (b, h, 0, 0)),
        ],
        out_shape=[
            jax.ShapeDtypeStruct((batch * seq, heads * dv), F32),
            jax.ShapeDtypeStruct((batch, heads, dk, dv), F32),
        ],
        scratch_shapes=[pltpu.VMEM((dk, dv), F32)],
        compiler_params=_cparams(("parallel", "parallel", "arbitrary")),
        name="retention",
    )(lg_rep, gain, qk, qk, vg, vg, state)


def _topk_rows(problems, k):
    t_ = problems[0][0].shape[1]
    krow = lax.broadcasted_iota(jnp.int32, (k, t_), 0)
    rows = [lax.broadcasted_iota(jnp.int32, s.shape, 0) for s, _ in problems]

    def body(r, carry):
        out = []
        for (s, vals, idxs), (_, payload), rw in zip(carry, problems, rows):
            m = jnp.max(s, axis=0, keepdims=True)
            am = jnp.min(jnp.where(s == m, rw, s.shape[0]), axis=0, keepdims=True)
            hit = rw == am
            if payload is None:
                pv = am
            else:
                pv = jnp.max(jnp.where(hit, payload, -1), axis=0, keepdims=True)
            vals = jnp.where(krow == r, m, vals)
            idxs = jnp.where(krow == r, pv, idxs)
            out.append((jnp.where(hit, -jnp.inf, s), vals, idxs))
        return tuple(out)

    init = tuple((s, jnp.zeros((k, t_), F32), jnp.zeros((k, t_), jnp.int32)) for s, _ in problems)
    res = lax.fori_loop(0, k, body, init)
    return [(vals, idxs) for _, vals, idxs in res]


def _peer_candidates(v1, i1, v2, i2):
    k = v1.shape[0]
    t_ = v1.shape[1]
    brow = lax.broadcasted_iota(jnp.int32, (SUBLANES, t_), 0)
    sums = [v1[0:1] + v2]
    ids = [i1[0:1] * N_KEYS + i2]
    tail = k // 2
    for a in range(1, tail):
        keep = brow < (k // (a + 1))
        sums.append(jnp.where(keep, v1[a:a + 1] + v2[0:SUBLANES], -jnp.inf))
        ids.append(i1[a:a + 1] * N_KEYS + i2[0:SUBLANES])
    sums.append(v1[tail:k] + v2[0:1])
    ids.append(i1[tail:k] * N_KEYS + i2[0:1])
    return jnp.concatenate(sums, axis=0), jnp.concatenate(ids, axis=0)


def _peer_route_kernel(x_ref, g_ref, wq_ref, keys_ref, xn_ref, idx_ref, gate_ref):
    x = x_ref[...]
    ms = jnp.mean(x * x, axis=-1, keepdims=True)
    xn = (x * lax.rsqrt(ms + NORM_EPS)) * g_ref[...]
    xn_ref[...] = xn
    q = jnp.dot(xn.astype(BF16), wq_ref[...], preferred_element_type=F32)
    st = lax.dot_general(keys_ref[...], q.astype(BF16), (((1,), (1,)), ((), ())),
                         preferred_element_type=F32)
    kk = PEER_TOPK
    cands = []
    pair_heads = 1
    for h0 in range(0, PEER_HEADS, pair_heads):
        slabs = [(st[r * N_KEYS:(r + 1) * N_KEYS], None)
                 for r in range(2 * h0, 2 * (h0 + pair_heads))]
        tops = _topk_rows(slabs, kk)
        for u in range(pair_heads):
            (v1, i1), (v2, i2) = tops[2 * u], tops[2 * u + 1]
            cands.append(_peer_candidates(v1, i1, v2, i2))
    group = 4
    for h0 in range(0, PEER_HEADS, group):
        picked = _topk_rows(cands[h0:h0 + group], kk)
        for h, (sc, idx) in zip(range(h0, h0 + group), picked):
            e = jnp.exp(sc - sc[0:1])
            gate = e / jnp.sum(e, axis=0, keepdims=True)
            idx_ref[h * kk:(h + 1) * kk, :] = idx
            gate_ref[h * kk:(h + 1) * kk, :] = gate


def _peer_route(h, gain, wq, keys_bd, *, tb=128):
    n, d = h.shape
    hk = PEER_HEADS * PEER_TOPK
    return pl.pallas_call(
        _peer_route_kernel,
        grid=(n // tb,),
        in_specs=[
            pl.BlockSpec((tb, d), lambda i: (i, 0)),
            pl.BlockSpec((1, d), lambda i: (0, 0)),
            pl.BlockSpec(wq.shape, lambda i: (0, 0)),
            pl.BlockSpec(keys_bd.shape, lambda i: (0, 0)),
        ],
        out_specs=[
            pl.BlockSpec((tb, d), lambda i: (i, 0)),
            pl.BlockSpec((hk, tb), lambda i: (0, i)),
            pl.BlockSpec((hk, tb), lambda i: (0, i)),
        ],
        out_shape=[
            jax.ShapeDtypeStruct((n, d), F32),
            jax.ShapeDtypeStruct((hk, n), jnp.int32),
            jax.ShapeDtypeStruct((hk, n), F32),
        ],
        compiler_params=_cparams(("parallel",)),
        name="peer_route",
    )(h, gain.reshape(1, d).astype(F32), wq, keys_bd)


def _gelu_tanh(x):
    c = 0.7978845608028654
    return 0.5 * x * (1.0 + jnp.tanh(c * (x + 0.044715 * (x * x * x))))


def _peer_expert_kernel(idx_ref, gate_ref, xn_ref, h_ref, tab_ref, o_ref, buf0_ref, buf1_ref,
                        sem_ref, *, d, n_blocks):
    g = pl.program_id(0)
    tb, hk = idx_ref.shape
    pairs = tb * hk
    half = d // 2
    wr = half // LANES
    rows_per = 2 * wr
    pitch = rows_per + PEER_PITCH_PAD
    cw = 2 if wr % 2 == 0 else 1
    bufs = (buf0_ref, buf1_ref)

    assert sum(PEER_GROUP_PAIRS) == pairs and len(PEER_STARTS_BEFORE_WAIT) == len(PEER_GROUP_PAIRS)
    n_grp = len(PEER_GROUP_PAIRS)
    g_lo = [sum(PEER_GROUP_PAIRS[:k]) for k in range(n_grp)]
    group_of = [k for k, n in enumerate(PEER_GROUP_PAIRS) for _ in range(n)]
    kc = cw * LANES
    n_chunks = wr // cw

    def issue(slot, p0, p1):
        for p in range(p0, p1):
            pltpu.make_async_copy(tab_ref.at[idx_ref[p // hk, p % hk]],
                                  bufs[slot].at[pl.ds(p * pitch, rows_per), :],
                                  sem_ref.at[slot, group_of[p]]).start(priority=p % 2)

    def wait(slot, grp):
        part = bufs[slot].at[pl.ds(0, PEER_GROUP_PAIRS[grp] * rows_per), :]
        pltpu.make_async_copy(part, part, sem_ref.at[slot, grp]).wait()

    def words(slot, r0, p0, n):
        parts = [pltpu.bitcast(bufs[slot][pl.ds(p0 * pitch + r0 + r, n, stride=pitch), :], BF16)
                 for r in range(cw)]
        return parts[0] if cw == 1 else jnp.concatenate(parts, axis=1)

    def split_x():
        x = xn_ref[...]
        return jnp.concatenate([x[:, :half], x[:, half:]], axis=0).astype(BF16)

    def first_stage(slot, grp, x2):
        gp = PEER_GROUP_PAIRS[grp]
        r = jnp.zeros((2 * tb, 2 * gp), F32)
        for c in range(n_chunks):
            r = r + lax.dot_general(x2[:, c * kc:(c + 1) * kc], words(slot, c * cw, g_lo[grp], gp),
                                    (((1,), (1,)), ((), ())), preferred_element_type=F32)
        return r

    def coefficients(rs, p0):
        r = rs[0] if len(rs) == 1 else jnp.concatenate(rs, axis=1)
        n2 = r.shape[1]
        act = r[:tb] + pltpu.roll(r[tb:], n2 - 1, axis=1)
        gate = jnp.concatenate([gate_ref[...]] * tb, axis=1)[:, 2 * p0:2 * p0 + n2]
        row = lax.broadcasted_iota(jnp.int32, (tb, n2), 0)
        lane = lax.broadcasted_iota(jnp.int32, (tb, n2), 1) + 2 * p0
        own = ((lane >= row * (2 * hk)) & (lane < (row + 1) * (2 * hk))
               & (jnp.bitwise_and(lane, 1) == 0))
        coef_lo = jnp.where(own, gate * _gelu_tanh(act), 0.0)
        coef_hi = pltpu.roll(coef_lo, 1, axis=1)
        return jnp.concatenate([coef_lo, coef_hi], axis=0).astype(BF16)

    def second_stage(slot, parts, before_chunk=lambda c: None):
        for c in range(n_chunks):
            before_chunk(c)
            out = jnp.zeros((2 * tb, kc), F32)
            for coef, p0, n in parts:
                out = out + jnp.dot(coef, words(slot, wr + c * cw, p0, n),
                                    preferred_element_type=F32)
            lo = slice(c * kc, (c + 1) * kc)
            hi = slice(half + c * kc, half + (c + 1) * kc)
            o_ref[:, lo] = h_ref[:, lo] + out[:tb]
            o_ref[:, hi] = h_ref[:, hi] + out[tb:]

    @pl.when(g == 0)
    def _():
        issue(0, 0, pairs)

    cuts = [min(pairs, sum(PEER_STARTS_BEFORE_WAIT[:k])) for k in range(n_grp + 1)] + [pairs]
    p_last = g_lo[n_grp - 1]

    def block(slot, issue_part, before_chunk):
        x2 = split_x()
        rs = []
        for grp in range(n_grp - 1):
            issue_part(grp)
            wait(slot, grp)
            rs.append(first_stage(slot, grp, x2))
        parts = [(coefficients(rs, 0), 0, p_last)] if rs else []
        issue_part(n_grp - 1)
        wait(slot, n_grp - 1)
        tail = coefficients([first_stage(slot, n_grp - 1, x2)], p_last)
        second_stage(slot, parts + [(tail, p_last, pairs - p_last)], before_chunk)

    for par in range(2):
        @pl.when((g >= 1) & (g < n_blocks) & (g % 2 == par))
        def _():
            late0 = cuts[n_grp]
            per = -(-(pairs - late0) // n_chunks)
            block(1 - par,
                  lambda grp: issue(par, cuts[grp], cuts[grp + 1]),
                  lambda c: issue(par, min(pairs, late0 + c * per),
                                  min(pairs, late0 + (c + 1) * per)))

    @pl.when(g == n_blocks)
    def _():
        block((n_blocks - 1) % 2, lambda grp: None, lambda c: None)


def _peer_experts(idx, gate2, xn, h, table):
    n, d = h.shape
    hk = idx.shape[1]
    tb = PEER_TOKENS_PER_STEP
    n_blocks = n // tb
    rows_per = table.shape[1]
    assert rows_per == d // LANES and n % tb == 0
    kern = functools.partial(_peer_expert_kernel, d=d, n_blocks=n_blocks)

    def cur(g):
        return (jnp.maximum(g - 1, 0), 0)

    slot = pltpu.VMEM((tb * hk * (rows_per + PEER_PITCH_PAD), LANES), jnp.uint32)
    return pl.pallas_call(
        kern,
        grid=(n_blocks + 1,),
        in_specs=[
            pl.BlockSpec((tb, hk), lambda g: (jnp.minimum(g, n_blocks - 1), 0),
                         memory_space=pltpu.SMEM),
            pl.BlockSpec((tb, 2 * hk), cur),
            pl.BlockSpec((tb, d), cur),
            pl.BlockSpec((tb, d), cur),
            pl.BlockSpec(memory_space=pl.ANY),
        ],
        out_specs=pl.BlockSpec((tb, d), cur),
        out_shape=jax.ShapeDtypeStruct((n, d), F32),
        scratch_shapes=[slot, slot, pltpu.SemaphoreType.DMA((2, len(PEER_GROUP_PAIRS)))],
        compiler_params=_cparams(("arbitrary",)),
        name="peer_experts",
    )(idx, gate2, xn, h, table)


def _pack_bf16_halves(w):
    e, d = w.shape
    bits = lax.bitcast_convert_type(w.astype(BF16), jnp.uint16).astype(jnp.uint32)
    words = bits[:, :d // 2] | (bits[:, d // 2:] << 16)
    return words.reshape(e, d // 2 // LANES, LANES)


def _peer_keys_blockdiag(keys1, keys2):
    half = PEER_KEY_DIM // 2
    eye = jnp.eye(PEER_HEADS * 2, dtype=F32)
    keys = jnp.stack([keys1, keys2]).astype(F32)
    keys = jnp.tile(keys, (PEER_HEADS, 1, 1))
    bd = jnp.einsum("gnc,gf->gnfc", keys, eye)
    return bd.reshape(PEER_HEADS * 2 * N_KEYS, PEER_HEADS * 2 * half).astype(BF16)


def _peer_layer(h, gain, wq, keys1, keys2, u, v):
    xn, idx_t, gate_t = _peer_route(h, gain, wq.astype(BF16), _peer_keys_blockdiag(keys1, keys2))
    table = jnp.concatenate([_pack_bf16_halves(u), _pack_bf16_halves(v)], axis=1)
    gate2 = jnp.repeat(gate_t.T, 2, axis=1)
    return _peer_experts(idx_t.T, gate2, xn, h, table)


def _rope_tables(positions, head_dim):
    half = head_dim // 2
    inv = ROPE_BASE ** (-jnp.arange(half, dtype=F32) / half)
    ang = positions.astype(F32)[:, None] * inv[None, :]
    return jnp.cos(ang), jnp.sin(ang)


def kernel(x_prompt, x_sample, cache_k, cache_v, state_ret, page_table, norm_mix, norm_ffn,
           sb_wqkv, sb_q_gain, sb_k_gain, sb_bias, sb_wo, ret_wqkvg, ret_norm_gain, ret_wo,
           peer_wq, peer_keys1, peer_keys2, peer_u, peer_v):
    bp, sp, d = x_prompt.shape
    bd, sd, _ = x_sample.shape
    n_p = bp * sp
    n_s = bd * sd
    n_pages = page_table.shape[1]
    past = n_pages * PAGE_SIZE
    dh = d // SB_HEADS
    n_pool = cache_k.shape[1]

    h = jnp.concatenate([x_prompt.reshape(n_p, d), x_sample.reshape(n_s, d)], axis=0)

    wqkv = sb_wqkv[0].astype(BF16)
    qk_gain = jnp.concatenate([jnp.tile(sb_q_gain[0], SB_HEADS),
                               jnp.tile(sb_k_gain[0], SB_HEADS)]).reshape(1, 2 * d).astype(F32)
    qk = _norm_matmul(h, wqkv[:, :2 * d], gain=norm_mix[0], epilogue=_headnorm_epilogue,
                      extras=(qk_gain,),
                      extra_specs=(lambda tm, tn: pl.BlockSpec((1, tn), lambda i, j: (0, j)),),
                      name="sb_qk_proj")
    v = _norm_matmul(h, wqkv[:, 2 * d:], gain=norm_mix[0], name="sb_v_proj")

    bias = sb_bias[0].astype(F32)
    bias_rep = jnp.broadcast_to(bias[:, None, None], (SB_HEADS, 1, LANES))
    att_p = _sb_prompt(qk, v, bias_rep, batch=bp, seq=sp, heads=SB_HEADS)
    bias_rows = jnp.broadcast_to(jnp.repeat(bias, sd)[:, None], (SB_HEADS * sd, LANES))
    att_s = _sb_sample(qk, v,
                       cache_k[0], cache_v[0],
                       page_table, bias_rows, row0=n_p, dec_batch=bd, t=sd, heads=SB_HEADS)
    att = jnp.concatenate([att_p, att_s], axis=0)
    h = _norm_matmul(att, sb_wo[0].astype(BF16), res=h, name="sb_wo")
    h = _peer_layer(h, norm_ffn[0], peer_wq[0], peer_keys1[0], peer_keys2[0], peer_u[0], peer_v[0])

    k_all = qk[:, d:]
    new_k_prompt = k_all[:n_p].reshape(1, bp, sp, SB_HEADS, dh)
    new_k_sample = k_all[n_p:].reshape(1, bd, sd, SB_HEADS, dh)
    new_v_prompt = v[:n_p].reshape(1, bp, sp, SB_HEADS, dh)
    new_v_sample = v[n_p:].reshape(1, bd, sd, SB_HEADS, dh)

    dk = d // RET_HEADS
    dv = 2 * d // RET_HEADS
    hq = RET_HEADS * dk
    hv = RET_HEADS * dv
    w = ret_wqkvg[0].astype(BF16)
    pos = jnp.concatenate([jnp.tile(jnp.arange(sp), bp), jnp.tile(past + jnp.arange(sd), bd)])
    cos, sin = _rope_tables(pos, dk)
    rope = functools.partial(_rope_epilogue, head_dim=dk, q_cols=hq, k_scale=dk ** -0.5)

    def half_spec(tm, tn):
        return pl.BlockSpec((tm, dk // 2), lambda i, j: (i, 0))

    rqk = _norm_matmul(h, w[:, :2 * hq], gain=norm_mix[1], epilogue=rope, extras=(cos, sin),
                       extra_specs=(half_spec, half_spec), name="ret_qk_proj")
    rvg = _norm_matmul(h, w[:, 2 * hq:], gain=norm_mix[1], name="ret_vg_proj")

    log_g = jnp.log1p(-jnp.exp2(-5.0 - jnp.arange(RET_HEADS, dtype=F32)))
    lg_rep = jnp.broadcast_to(log_g[:, None, None], (RET_HEADS, 1, LANES))
    ret_gain = ret_norm_gain[0].reshape(RET_HEADS, 1, dv).astype(F32)
    zero_state = jnp.zeros((bp, RET_HEADS, dk, dv), F32)
    o_p, r_p = _retention(rqk, rvg, zero_state, lg_rep, ret_gain,
                          row0=0, batch=bp, seq=sp, heads=RET_HEADS)
    o_s, r_s = _retention(rqk, rvg, state_ret[0], lg_rep, ret_gain,
                          row0=n_p, batch=bd, seq=sd, heads=RET_HEADS)
    o = jnp.concatenate([o_p, o_s], axis=0)
    h = _norm_matmul(o, ret_wo[0].astype(BF16), res=h, name="ret_wo")
    h = _peer_layer(h, norm_ffn[1], peer_wq[1], peer_keys1[1], peer_keys2[1], peer_u[1], peer_v[1])

    return (h[:n_p].reshape(bp, sp, d), h[n_p:].reshape(bd, sd, d),
            new_k_prompt, new_v_prompt, new_k_sample, new_v_sample,
            r_p[None], r_s[None])
```

```python
import functools

import jax
import jax.numpy as jnp
from jax import lax
from jax.experimental import pallas as pl
from jax.experimental.pallas import tpu as pltpu

F32 = jnp.float32
BF16 = jnp.bfloat16

SB_HEADS = 16
RET_HEADS = 8
PEER_HEADS = 8
PEER_KEY_DIM = 128
N_KEYS = 128
PEER_TOPK = 16
PAGE_SIZE = 128
Q_BLOCK = 128
RET_CHUNK = 128
ROPE_BASE = 10000.0
NORM_EPS = 1e-6

LANES = 128
SUBLANES = 8
BF16_ROWS = 16
VMEM_LIMIT = 52 * 1024 * 1024

PEER_TOKENS_PER_STEP = 8
PEER_PITCH_PAD = 4
PEER_GROUP_PAIRS = (256, 256, 256, 192, 64)
PEER_STARTS_BEFORE_WAIT = (0, 150, 150, 150, 276)
SB_PROMPT_Q_ROWS = 512
SB_HEAD_PITCH_PAD = 8
SB_PAGE_LOOKAHEAD = 2
MM_ROW_TILE_BYTES = 8 * 1024 * 1024


def _cparams(sem, vmem=VMEM_LIMIT):
    return pltpu.CompilerParams(dimension_semantics=sem, vmem_limit_bytes=vmem)


def _mm_kernel(*refs, has_gain, has_res, n_extra, epilogue, row_chunk):
    it = iter(refs)
    x_ref = next(it)
    g_ref = next(it) if has_gain else None
    w_ref = next(it)
    r_ref = next(it) if has_res else None
    extra_refs = [next(it) for _ in range(n_extra)]
    o_ref = next(it)
    xn_ref = next(it)
    tm = x_ref.shape[0]

    @pl.when(pl.program_id(1) == 0)
    def _():
        def chunk(c, carry):
            rows = pl.ds(pl.multiple_of(c * row_chunk, row_chunk), row_chunk)
            xv = x_ref[rows, :].astype(F32)
            if has_gain:
                ms = jnp.mean(xv * xv, axis=-1, keepdims=True)
                xv = (xv * lax.rsqrt(ms + NORM_EPS)) * g_ref[...]
            xn_ref[rows, :] = xv.astype(BF16)
            return carry
        lax.fori_loop(0, tm // row_chunk, chunk, 0)

    acc = jnp.dot(xn_ref[...], w_ref[...], preferred_element_type=F32)
    if has_res:
        acc = acc + r_ref[...]
    if epilogue is None:
        o_ref[...] = acc.astype(o_ref.dtype)
    else:
        epilogue(acc, o_ref, extra_refs, pl.program_id(1))


def _norm_matmul(x, w, *, gain=None, res=None, epilogue=None, extras=(), extra_specs=(),
                 tm=512, tn=512, name="mm"):
    n, k = x.shape
    m = w.shape[1]
    tn = min(tn, m)
    if n % (2 * tm) == 0 and 2 * tm * k * 4 <= MM_ROW_TILE_BYTES:
        tm = 2 * tm
    assert n % tm == 0 and m % tn == 0, (n, tm, m, tn)
    extra_specs = [make(tm, tn) for make in extra_specs]
    in_specs = [pl.BlockSpec((tm, k), lambda i, j: (i, 0))]
    args = [x]
    if gain is not None:
        in_specs.append(pl.BlockSpec((1, k), lambda i, j: (0, 0)))
        args.append(gain.reshape(1, k).astype(F32))
    in_specs.append(pl.BlockSpec((k, tn), lambda i, j: (0, j)))
    args.append(w)
    if res is not None:
        in_specs.append(pl.BlockSpec((tm, tn), lambda i, j: (i, j)))
        args.append(res)
    in_specs.extend(extra_specs)
    args.extend(extras)
    kern = functools.partial(_mm_kernel, has_gain=gain is not None, has_res=res is not None,
                             n_extra=len(extras), epilogue=epilogue, row_chunk=min(64, tm))
    return pl.pallas_call(
        kern,
        grid=(n // tm, m // tn),
        in_specs=in_specs,
        out_specs=pl.BlockSpec((tm, tn), lambda i, j: (i, j)),
        out_shape=jax.ShapeDtypeStruct((n, m), F32),
        scratch_shapes=[pltpu.VMEM((tm, k), BF16)],
        compiler_params=_cparams(("parallel", "arbitrary")),
        name=name,
    )(*args)


def _headnorm_epilogue(acc, o_ref, extra_refs, j):
    gain_ref, = extra_refs
    for g in range(acc.shape[1] // LANES):
        sl = slice(g * LANES, (g + 1) * LANES)
        y = acc[:, sl]
        ms = jnp.mean(y * y, axis=-1, keepdims=True)
        o_ref[:, sl] = (y * lax.rsqrt(ms + NORM_EPS)) * gain_ref[:, sl]


def _rope_epilogue(acc, o_ref, extra_refs, j, *, head_dim, q_cols, k_scale):
    cos_ref, sin_ref = extra_refs
    cos = cos_ref[...]
    sin = sin_ref[...]
    half = head_dim // 2
    tn = acc.shape[1]
    for g in range(tn // head_dim):
        scale = jnp.where(j * tn + g * head_dim >= q_cols, jnp.float32(k_scale), jnp.float32(1.0))
        x1 = acc[:, g * head_dim:g * head_dim + half]
        x2 = acc[:, g * head_dim + half:(g + 1) * head_dim]
        o_ref[:, g * head_dim:g * head_dim + half] = (x1 * cos - x2 * sin) * scale
        o_ref[:, g * head_dim + half:(g + 1) * head_dim] = (x1 * sin + x2 * cos) * scale


def _softplus(z):
    return jnp.maximum(z, 0.0) + jnp.log1p(jnp.exp(-jnp.abs(z)))


def _suffix_sum_weights(n):
    row = lax.broadcasted_iota(jnp.int32, (n, 2 * n), 0)
    col = lax.broadcasted_iota(jnp.int32, (n, 2 * n), 1)
    return jnp.where((col >= n) | (row > col), 1.0, 0.0).astype(BF16)


def _split_dot(x, w):
    hi = x.astype(BF16)
    lo = (x - hi.astype(F32)).astype(BF16)
    return jnp.dot(jnp.concatenate([hi, lo], axis=1), jnp.concatenate([w, w], axis=0),
                   preferred_element_type=F32)


def _sb_tile(z, valid, carry, lo_w):
    n = z.shape[1]
    sp = _softplus(z)
    log_beta = z - sp
    log_rest = -sp
    if valid is not None:
        log_rest = jnp.where(valid, log_rest, 0.0)
    cs = _split_dot(log_rest, lo_w)
    between = carry + cs[:, :n]
    a = jnp.exp(log_beta + between)
    if valid is not None:
        a = jnp.where(valid, a, 0.0)
    return a, carry + cs[:, n:]


def _sb_prompt_kernel(bias_ref, q_ref, k_ref, v_ref, o_ref, *, scale):
    i = pl.program_id(2)
    tq = q_ref.shape[0]
    tk = LANES
    sub = tq // tk
    q = q_ref[...].astype(BF16)
    bias = bias_ref[...]
    lo_w = _suffix_sum_weights(tk)
    row = lax.broadcasted_iota(jnp.int32, (tq, tk), 0)
    col = lax.broadcasted_iota(jnp.int32, (tq, tk), 1)

    grp = next(n for n in (4, 2, 1) if sub % n == 0)

    def tiles(kg, acc, carry, valids):
        rows = pl.ds(pl.multiple_of(kg * (grp * tk), grp * tk), grp * tk)
        k = k_ref[rows, :].astype(BF16)
        v = v_ref[rows, :].astype(BF16)
        z = lax.dot_general(q, k, (((1,), (1,)), ((), ())), preferred_element_type=F32)
        parts = [None] * grp
        for u in reversed(range(grp)):
            zu = z[:, u * tk:(u + 1) * tk] * scale + bias
            parts[u], carry = _sb_tile(zu, valids[u], carry, lo_w)
        a = parts[0] if grp == 1 else jnp.concatenate(parts, axis=1)
        acc = acc + jnp.dot(a.astype(BF16), v, preferred_element_type=F32)
        return acc, carry

    acc = jnp.zeros((tq, v_ref.shape[1]), F32)
    carry = jnp.zeros((tq, tk), F32)
    for dg in reversed(range(sub // grp)):
        valids = [col + (dg * grp + u) * tk < row for u in range(grp)]
        acc, carry = tiles(i * (sub // grp) + dg, acc, carry, valids)

    def body(jj, c):
        return tiles(i * (sub // grp) - 1 - jj, c[0], c[1], [None] * grp)

    acc, carry = lax.fori_loop(0, i * (sub // grp), body, (acc, carry))
    o_ref[...] = acc.astype(o_ref.dtype)


def _sb_prompt(qk, v, bias_rep, *, batch, seq, heads):
    dh = LANES
    d = heads * dh
    tq = SB_PROMPT_Q_ROWS if seq % SB_PROMPT_Q_ROWS == 0 else LANES
    nq = seq // tq
    kern = functools.partial(_sb_prompt_kernel, scale=dh ** -0.5)
    return pl.pallas_call(
        kern,
        grid=(batch, heads, nq),
        in_specs=[
            pl.BlockSpec((None, 1, LANES), lambda b, h, i: (h, 0, 0)),
            pl.BlockSpec((tq, dh), lambda b, h, i: (b * nq + i, h)),
            pl.BlockSpec((seq, dh), lambda b, h, i: (b, heads + h)),
            pl.BlockSpec((seq, dh), lambda b, h, i: (b, h)),
        ],
        out_specs=pl.BlockSpec((tq, dh), lambda b, h, i: (b * nq + i, h)),
        out_shape=jax.ShapeDtypeStruct((batch * seq, d), F32),
        compiler_params=_cparams(("parallel", "parallel", "arbitrary")),
        name="sb_prompt",
    )(bias_rep, qk, qk, v)


def _sb_sample_kernel(pt_ref, bias_ref, q_ref, kn_ref, vn_ref, ck_ref, cv_ref, o_ref,
                      acc_ref, carry_ref, kbuf_ref, vbuf_ref, sem_ref, *, heads, n_pages, scale):
    b = pl.program_id(0)
    s = pl.program_id(1)
    t = q_ref.shape[0]
    pitch = heads + SB_HEAD_PITCH_PAD

    n_slots = SB_PAGE_LOOKAHEAD + 1
    total_pages = pl.num_programs(0) * n_pages

    def page_copy(cache_ref, buf_ref, page, slot, which):
        return pltpu.make_async_copy(
            cache_ref.at[page],
            buf_ref.at[pl.ds(slot * PAGE_SIZE, PAGE_SIZE), pl.ds(0, heads), :],
            sem_ref.at[which, slot])

    def start_page(k):
        @pl.when(k < total_pages)
        def _():
            page = pt_ref[k // n_pages, n_pages - 1 - k % n_pages]
            slot = k % n_slots
            page_copy(ck_ref, kbuf_ref, page, slot, 0).start()
            page_copy(cv_ref, vbuf_ref, page, slot, 1).start()
    dh = LANES
    hq = heads * t
    bias = bias_ref[...]
    lo_w = _suffix_sum_weights(LANES)

    def q_head(h):
        qh = q_ref[:, h * dh:(h + 1) * dh]
        pad = jnp.zeros((BF16_ROWS - t, dh), F32)
        return jnp.concatenate([qh, pad], axis=0).astype(BF16)

    def attend(k_of, v_of, valid):
        zs = []
        for h in range(heads):
            zh = lax.dot_general(q_head(h), k_of(h), (((1,), (1,)), ((), ())),
                                 preferred_element_type=F32)
            zs.append(zh[:t])
        z = jnp.concatenate(zs, axis=0) * scale + bias
        a, carry = _sb_tile(z, valid, carry_ref[...], lo_w)
        carry_ref[...] = carry
        for h in range(heads):
            ah = a[h * t:(h + 1) * t]
            ah = jnp.concatenate([ah, jnp.zeros((BF16_ROWS - t, LANES), F32)], axis=0)
            oh = jnp.dot(ah.astype(BF16), v_of(h), preferred_element_type=F32)
            acc_ref[:, h * dh:(h + 1) * dh] += oh[:t]

    @pl.when((s == 0) & (b == 0))
    def _():
        for k0 in range(SB_PAGE_LOOKAHEAD):
            start_page(jnp.int32(k0))

    @pl.when(s == 0)
    def _():
        acc_ref[...] = jnp.zeros_like(acc_ref)
        carry_ref[...] = jnp.zeros_like(carry_ref)
        row = lax.broadcasted_iota(jnp.int32, (hq, LANES), 0)
        col = lax.broadcasted_iota(jnp.int32, (hq, LANES), 1)
        valid = col < lax.rem(row, t)
        zpad = jnp.zeros((LANES - t, dh), F32)

        def k_of(h):
            return jnp.concatenate([kn_ref[:, h * dh:(h + 1) * dh], zpad], axis=0).astype(BF16)

        def v_of(h):
            return jnp.concatenate([vn_ref[:, h * dh:(h + 1) * dh], zpad], axis=0).astype(BF16)

        attend(k_of, v_of, valid)

    @pl.when(s > 0)
    def _():
        k = b * n_pages + s - 1
        slot = k % n_slots
        start_page(k + SB_PAGE_LOOKAHEAD)
        page_copy(ck_ref, kbuf_ref, 0, slot, 0).wait()
        page_copy(cv_ref, vbuf_ref, 0, slot, 1).wait()
        flat = (n_slots * PAGE_SIZE * pitch, LANES)

        def k_of(h):
            return kbuf_ref.reshape(*flat)[pl.ds(slot * PAGE_SIZE * pitch + h, PAGE_SIZE,
                                                 stride=pitch), :].astype(BF16)

        def v_of(h):
            return vbuf_ref.reshape(*flat)[pl.ds(slot * PAGE_SIZE * pitch + h, PAGE_SIZE,
                                                 stride=pitch), :].astype(BF16)

        attend(k_of, v_of, None)

    @pl.when(s == pl.num_programs(1) - 1)
    def _():
        o_ref[...] = acc_ref[...]


def _sb_sample(qk, v, cache_k, cache_v, page_table, bias_rows, *, row0, dec_batch, t, heads):
    dh = LANES
    d = heads * dh
    n_pages = page_table.shape[1]
    rb0 = row0 // t
    n_slots = SB_PAGE_LOOKAHEAD + 1
    ring = pltpu.VMEM((n_slots * PAGE_SIZE, heads + SB_HEAD_PITCH_PAD, dh), F32)

    kern = functools.partial(_sb_sample_kernel, heads=heads, n_pages=n_pages, scale=dh ** -0.5)
    grid_spec = pltpu.PrefetchScalarGridSpec(
        num_scalar_prefetch=1,
        grid=(dec_batch, n_pages + 1),
        in_specs=[
            pl.BlockSpec((heads * t, LANES), lambda b, s, pt: (0, 0)),
            pl.BlockSpec((t, d), lambda b, s, pt: (rb0 + b, 0)),
            pl.BlockSpec((t, d), lambda b, s, pt: (rb0 + b, 1)),
            pl.BlockSpec((t, d), lambda b, s, pt: (rb0 + b, 0)),
            pl.BlockSpec(memory_space=pl.ANY),
            pl.BlockSpec(memory_space=pl.ANY),
        ],
        out_specs=pl.BlockSpec((t, d), lambda b, s, pt: (b, 0)),
        scratch_shapes=[pltpu.VMEM((t, d), F32), pltpu.VMEM((heads * t, LANES), F32),
                        ring, ring, pltpu.SemaphoreType.DMA((2, n_slots))],
    )
    return pl.pallas_call(
        kern,
        grid_spec=grid_spec,
        out_shape=jax.ShapeDtypeStruct((dec_batch * t, d), F32),
        compiler_params=_cparams(("arbitrary", "arbitrary")),
        name="sb_sample",
    )(page_table, bias_rows, qk, qk, v, cache_k, cache_v)


def _retention_kernel(lg_ref, gain_ref, q_ref, k_ref, v_ref, g_ref, s_ref, o_ref, so_ref, r_ref,
                      *, chunk_len):
    c = pl.program_id(2)
    rows_in = q_ref.shape[0]
    cp = LANES
    assert rows_in <= cp
    dk = q_ref.shape[1]
    dv = v_ref.shape[1]

    @pl.when(c == 0)
    def _():
        r_ref[...] = s_ref[...]

    def load(ref):
        x = ref[...]
        if rows_in < cp:
            x = jnp.concatenate([x, jnp.zeros((cp - rows_in, x.shape[1]), F32)], axis=0)
        return x

    lg = lg_ref[...]
    row = lax.broadcasted_iota(jnp.int32, (cp, LANES), 0).astype(F32)
    col = lax.broadcasted_iota(jnp.int32, (cp, LANES), 1).astype(F32)
    decay_query = jnp.exp((row + 1.0) * lg)
    decay_key = jnp.exp((chunk_len - 1.0 - row) * lg)
    decay_chunk = jnp.exp(chunk_len * lg)
    diff = row - col
    decay_intra = jnp.where(diff >= 0, jnp.exp(jnp.maximum(diff, 0.0) * lg), 0.0)

    q = load(q_ref)
    k = load(k_ref)
    v = load(v_ref).astype(BF16)
    qb = q.astype(BF16)
    kb = k.astype(BF16)
    att = lax.dot_general(qb, kb, (((1,), (1,)), ((), ())), preferred_element_type=F32)
    att = att * decay_intra
    inner = jnp.dot(att.astype(BF16), v, preferred_element_type=F32)
    r = r_ref[...]
    cross = jnp.dot(qb, r.astype(BF16), preferred_element_type=F32)
    kd = jnp.concatenate([k[:, s * LANES:(s + 1) * LANES] * decay_key
                          for s in range(dk // LANES)], axis=1).astype(BF16)
    upd = lax.dot_general(kd, v, (((0,), (0,)), ((), ())), preferred_element_type=F32)
    for s in range(dv // LANES):
        sl = slice(s * LANES, (s + 1) * LANES)
        r_ref[:, sl] = decay_chunk * r[:, sl] + upd[:, sl]
    out = jnp.concatenate([inner[:, s * LANES:(s + 1) * LANES]
                           + cross[:, s * LANES:(s + 1) * LANES] * decay_query
                           for s in range(dv // LANES)], axis=1)[:rows_in]
    ms = jnp.mean(out * out, axis=-1, keepdims=True)
    o = (out * lax.rsqrt(ms + NORM_EPS)) * gain_ref[...]
    g = g_ref[...]
    o_ref[...] = (g * jax.nn.sigmoid(g)) * o

    @pl.when(c == pl.num_programs(2) - 1)
    def _():
        so_ref[...] = r_ref[...]


def _retention(qk, vg, state, lg_rep, gain, *, row0, batch, seq, heads):
    dk = qk.shape[1] // (2 * heads)
    dv = vg.shape[1] // (2 * heads)
    assert dk % LANES == 0 and dv % LANES == 0
    c = RET_CHUNK if seq % RET_CHUNK == 0 else seq
    nc = seq // c
    rb0 = row0 // c

    def rows(b, h, ci):
        return rb0 + b * nc + ci

    kern = functools.partial(_retention_kernel, chunk_len=float(c))
    return pl.pallas_call(
        kern,
        grid=(batch, heads, nc),
        in_specs=[
            pl.BlockSpec((None, 1, LANES), lambda b, h, ci: (h, 0, 0)),
            pl.BlockSpec((None, 1, dv), lambda b, h, ci: (h, 0, 0)),
            pl.BlockSpec((c, dk), lambda b, h, ci: (rows(b, h, ci), h)),
            pl.BlockSpec((c, dk), lambda b, h, ci: (rows(b, h, ci), heads + h)),
            pl.BlockSpec((c, dv), lambda b, h, ci: (rows(b, h, ci), h)),
            pl.BlockSpec((c, dv), lambda b, h, ci: (rows(b, h, ci), heads + h)),
            pl.BlockSpec((None, None, dk, dv), lambda b, h, ci: (b, h, 0, 0)),
        ],
        out_specs=[
            pl.BlockSpec((c, dv), lambda b, h, ci: (b * nc + ci, h)),
            pl.BlockSpec((None, None, dk, dv), lambda b, h, ci: (b, h, 0, 0)),
        ],
        out_shape=[
            jax.ShapeDtypeStruct((batch * seq, heads * dv), F32),
            jax.ShapeDtypeStruct((batch, heads, dk, dv), F32),
        ],
        scratch_shapes=[pltpu.VMEM((dk, dv), F32)],
        compiler_params=_cparams(("parallel", "parallel", "arbitrary")),
        name="retention",
    )(lg_rep, gain, qk, qk, vg, vg, state)


def _topk_rows(problems, k):
    t_ = problems[0][0].shape[1]
    krow = lax.broadcasted_iota(jnp.int32, (k, t_), 0)
    rows = [lax.broadcasted_iota(jnp.int32, s.shape, 0) for s, _ in problems]

    def body(r, carry):
        out = []
        for (s, vals, idxs), (_, payload), rw in zip(carry, problems, rows):
            m = jnp.max(s, axis=0, keepdims=True)
            am = jnp.min(jnp.where(s == m, rw, s.shape[0]), axis=0, keepdims=True)
            hit = rw == am
            if payload is None:
                pv = am
            else:
                pv = jnp.max(jnp.where(hit, payload, -1), axis=0, keepdims=True)
            vals = jnp.where(krow == r, m, vals)
            idxs = jnp.where(krow == r, pv, idxs)
            out.append((jnp.where(hit, -jnp.inf, s), vals, idxs))
        return tuple(out)

    init = tuple((s, jnp.zeros((k, t_), F32), jnp.zeros((k, t_), jnp.int32)) for s, _ in problems)
    res = lax.fori_loop(0, k, body, init)
    return [(vals, idxs) for _, vals, idxs in res]


def _peer_candidates(v1, i1, v2, i2):
    k = v1.shape[0]
    t_ = v1.shape[1]
    brow = lax.broadcasted_iota(jnp.int32, (SUBLANES, t_), 0)
    sums = [v1[0:1] + v2]
    ids = [i1[0:1] * N_KEYS + i2]
    tail = k // 2
    for a in range(1, tail):
        keep = brow < (k // (a + 1))
        sums.append(jnp.where(keep, v1[a:a + 1] + v2[0:SUBLANES], -jnp.inf))
        ids.append(i1[a:a + 1] * N_KEYS + i2[0:SUBLANES])
    sums.append(v1[tail:k] + v2[0:1])
    ids.append(i1[tail:k] * N_KEYS + i2[0:1])
    return jnp.concatenate(sums, axis=0), jnp.concatenate(ids, axis=0)


def _peer_route_kernel(x_ref, g_ref, wq_ref, keys_ref, xn_ref, idx_ref, gate_ref):
    x = x_ref[...]
    ms = jnp.mean(x * x, axis=-1, keepdims=True)
    xn = (x * lax.rsqrt(ms + NORM_EPS)) * g_ref[...]
    xn_ref[...] = xn
    q = jnp.dot(xn.astype(BF16), wq_ref[...], preferred_element_type=F32)
    st = lax.dot_general(keys_ref[...], q.astype(BF16), (((1,), (1,)), ((), ())),
                         preferred_element_type=F32)
    kk = PEER_TOPK
    cands = []
    pair_heads = 1
    for h0 in range(0, PEER_HEADS, pair_heads):
        slabs = [(st[r * N_KEYS:(r + 1) * N_KEYS], None)
                 for r in range(2 * h0, 2 * (h0 + pair_heads))]
        tops = _topk_rows(slabs, kk)
        for u in range(pair_heads):
            (v1, i1), (v2, i2) = tops[2 * u], tops[2 * u + 1]
            cands.append(_peer_candidates(v1, i1, v2, i2))
    group = 4
    for h0 in range(0, PEER_HEADS, group):
        picked = _topk_rows(cands[h0:h0 + group], kk)
        for h, (sc, idx) in zip(range(h0, h0 + group), picked):
            e = jnp.exp(sc - sc[0:1])
            gate = e / jnp.sum(e, axis=0, keepdims=True)
            idx_ref[h * kk:(h + 1) * kk, :] = idx
            gate_ref[h * kk:(h + 1) * kk, :] = gate


def _peer_route(h, gain, wq, keys_bd, *, tb=128):
    n, d = h.shape
    hk = PEER_HEADS * PEER_TOPK
    return pl.pallas_call(
        _peer_route_kernel,
        grid=(n // tb,),
        in_specs=[
            pl.BlockSpec((tb, d), lambda i: (i, 0)),
            pl.BlockSpec((1, d), lambda i: (0, 0)),
            pl.BlockSpec(wq.shape, lambda i: (0, 0)),
            pl.BlockSpec(keys_bd.shape, lambda i: (0, 0)),
        ],
        out_specs=[
            pl.BlockSpec((tb, d), lambda i: (i, 0)),
            pl.BlockSpec((hk, tb), lambda i: (0, i)),
            pl.BlockSpec((hk, tb), lambda i: (0, i)),
        ],
        out_shape=[
            jax.ShapeDtypeStruct((n, d), F32),
            jax.ShapeDtypeStruct((hk, n), jnp.int32),
            jax.ShapeDtypeStruct((hk, n), F32),
        ],
        compiler_params=_cparams(("parallel",)),
        name="peer_route",
    )(h, gain.reshape(1, d).astype(F32), wq, keys_bd)


def _gelu_tanh(x):
    c = 0.7978845608028654
    return 0.5 * x * (1.0 + jnp.tanh(c * (x + 0.044715 * (x * x * x))))


def _peer_expert_kernel(idx_ref, gate_ref, xn_ref, h_ref, tab_ref, o_ref, buf0_ref, buf1_ref,
                        sem_ref, *, d, n_blocks):
    g = pl.program_id(0)
    tb, hk = idx_ref.shape
    pairs = tb * hk
    half = d // 2
    wr = half // LANES
    rows_per = 2 * wr
    pitch = rows_per + PEER_PITCH_PAD
    cw = 2 if wr % 2 == 0 else 1
    bufs = (buf0_ref, buf1_ref)

    assert sum(PEER_GROUP_PAIRS) == pairs and len(PEER_STARTS_BEFORE_WAIT) == len(PEER_GROUP_PAIRS)
    n_grp = len(PEER_GROUP_PAIRS)
    g_lo = [sum(PEER_GROUP_PAIRS[:k]) for k in range(n_grp)]
    group_of = [k for k, n in enumerate(PEER_GROUP_PAIRS) for _ in range(n)]
    kc = cw * LANES
    n_chunks = wr // cw

    def issue(slot, p0, p1):
        for p in range(p0, p1):
            pltpu.make_async_copy(tab_ref.at[idx_ref[p // hk, p % hk]],
                                  bufs[slot].at[pl.ds(p * pitch, rows_per), :],
                                  sem_ref.at[slot, group_of[p]]).start(priority=p % 2)

    def wait(slot, grp):
        part = bufs[slot].at[pl.ds(0, PEER_GROUP_PAIRS[grp] * rows_per), :]
        pltpu.make_async_copy(part, part, sem_ref.at[slot, grp]).wait()

    def words(slot, r0, p0, n):
        parts = [pltpu.bitcast(bufs[slot][pl.ds(p0 * pitch + r0 + r, n, stride=pitch), :], BF16)
                 for r in range(cw)]
        return parts[0] if cw == 1 else jnp.concatenate(parts, axis=1)

    def split_x():
        x = xn_ref[...]
        return jnp.concatenate([x[:, :half], x[:, half:]], axis=0).astype(BF16)

    def first_stage(slot, grp, x2):
        gp = PEER_GROUP_PAIRS[grp]
        r = jnp.zeros((2 * tb, 2 * gp), F32)
        for c in range(n_chunks):
            r = r + lax.dot_general(x2[:, c * kc:(c + 1) * kc], words(slot, c * cw, g_lo[grp], gp),
                                    (((1,), (1,)), ((), ())), preferred_element_type=F32)
        return r

    def coefficients(rs, p0):
        r = rs[0] if len(rs) == 1 else jnp.concatenate(rs, axis=1)
        n2 = r.shape[1]
        act = r[:tb] + pltpu.roll(r[tb:], n2 - 1, axis=1)
        gate = jnp.concatenate([gate_ref[...]] * tb, axis=1)[:, 2 * p0:2 * p0 + n2]
        row = lax.broadcasted_iota(jnp.int32, (tb, n2), 0)
        lane = lax.broadcasted_iota(jnp.int32, (tb, n2), 1) + 2 * p0
        own = ((lane >= row * (2 * hk)) & (lane < (row + 1) * (2 * hk))
               & (jnp.bitwise_and(lane, 1) == 0))
        coef_lo = jnp.where(own, gate * _gelu_tanh(act), 0.0)
        coef_hi = pltpu.roll(coef_lo, 1, axis=1)
        return jnp.concatenate([coef_lo, coef_hi], axis=0).astype(BF16)

    def second_stage(slot, parts, before_chunk=lambda c: None):
        for c in range(n_chunks):
            before_chunk(c)
            out = jnp.zeros((2 * tb, kc), F32)
            for coef, p0, n in parts:
                out = out + jnp.dot(coef, words(slot, wr + c * cw, p0, n),
                                    preferred_element_type=F32)
            lo = slice(c * kc, (c + 1) * kc)
            hi = slice(half + c * kc, half + (c + 1) * kc)
            o_ref[:, lo] = h_ref[:, lo] + out[:tb]
            o_ref[:, hi] = h_ref[:, hi] + out[tb:]

    @pl.when(g == 0)
    def _():
        issue(0, 0, pairs)

    cuts = [min(pairs, sum(PEER_STARTS_BEFORE_WAIT[:k])) for k in range(n_grp + 1)] + [pairs]
    p_last = g_lo[n_grp - 1]

    def block(slot, issue_part, before_chunk):
        x2 = split_x()
        rs = []
        for grp in range(n_grp - 1):
            issue_part(grp)
            wait(slot, grp)
            rs.append(first_stage(slot, grp, x2))
        parts = [(coefficients(rs, 0), 0, p_last)] if rs else []
        issue_part(n_grp - 1)
        wait(slot, n_grp - 1)
        tail = coefficients([first_stage(slot, n_grp - 1, x2)], p_last)
        second_stage(slot, parts + [(tail, p_last, pairs - p_last)], before_chunk)

    for par in range(2):
        @pl.when((g >= 1) & (g < n_blocks) & (g % 2 == par))
        def _():
            late0 = cuts[n_grp]
            per = -(-(pairs - late0) // n_chunks)
            block(1 - par,
                  lambda grp: issue(par, cuts[grp], cuts[grp + 1]),
                  lambda c: issue(par, min(pairs, late0 + c * per),
                                  min(pairs, late0 + (c + 1) * per)))

    @pl.when(g == n_blocks)
    def _():
        block((n_blocks - 1) % 2, lambda grp: None, lambda c: None)


def _peer_experts(idx, gate2, xn, h, table):
    n, d = h.shape
    hk = idx.shape[1]
    tb = PEER_TOKENS_PER_STEP
    n_blocks = n // tb
    rows_per = table.shape[1]
    assert rows_per == d // LANES and n % tb == 0
    kern = functools.partial(_peer_expert_kernel, d=d, n_blocks=n_blocks)

    def cur(g):
        return (jnp.maximum(g - 1, 0), 0)

    slot = pltpu.VMEM((tb * hk * (rows_per + PEER_PITCH_PAD), LANES), jnp.uint32)
    return pl.pallas_call(
        kern,
        grid=(n_blocks + 1,),
        in_specs=[
            pl.BlockSpec((tb, hk), lambda g: (jnp.minimum(g, n_blocks - 1), 0),
                         memory_space=pltpu.SMEM),
            pl.BlockSpec((tb, 2 * hk), cur),
            pl.BlockSpec((tb, d), cur),
            pl.BlockSpec((tb, d), cur),
            pl.BlockSpec(memory_space=pl.ANY),
        ],
        out_specs=pl.BlockSpec((tb, d), cur),
        out_shape=jax.ShapeDtypeStruct((n, d), F32),
        scratch_shapes=[slot, slot, pltpu.SemaphoreType.DMA((2, len(PEER_GROUP_PAIRS)))],
        compiler_params=_cparams(("arbitrary",)),
        name="peer_experts",
    )(idx, gate2, xn, h, table)


def _pack_bf16_halves(w):
    e, d = w.shape
    bits = lax.bitcast_convert_type(w.astype(BF16), jnp.uint16).astype(jnp.uint32)
    words = bits[:, :d // 2] | (bits[:, d // 2:] << 16)
    return words.reshape(e, d // 2 // LANES, LANES)


def _peer_keys_blockdiag(keys1, keys2):
    half = PEER_KEY_DIM // 2
    eye = jnp.eye(PEER_HEADS * 2, dtype=F32)
    keys = jnp.stack([keys1, keys2]).astype(F32)
    keys = jnp.tile(keys, (PEER_HEADS, 1, 1))
    bd = jnp.einsum("gnc,gf->gnfc", keys, eye)
    return bd.reshape(PEER_HEADS * 2 * N_KEYS, PEER_HEADS * 2 * half).astype(BF16)


def _peer_layer(h, gain, wq, keys1, keys2, u, v):
    xn, idx_t, gate_t = _peer_route(h, gain, wq.astype(BF16), _peer_keys_blockdiag(keys1, keys2))
    table = jnp.concatenate([_pack_bf16_halves(u), _pack_bf16_halves(v)], axis=1)
    gate2 = jnp.repeat(gate_t.T, 2, axis=1)
    return _peer_experts(idx_t.T, gate2, xn, h, table)


def _rope_tables(positions, head_dim):
    half = head_dim // 2
    inv = ROPE_BASE ** (-jnp.arange(half, dtype=F32) / half)
    ang = positions.astype(F32)[:, None] * inv[None, :]
    return jnp.cos(ang), jnp.sin(ang)


def kernel(x_prompt, x_sample, cache_k, cache_v, state_ret, page_table, norm_mix, norm_ffn,
           sb_wqkv, sb_q_gain, sb_k_gain, sb_bias, sb_wo, ret_wqkvg, ret_norm_gain, ret_wo,
           peer_wq, peer_keys1, peer_keys2, peer_u, peer_v):
    bp, sp, d = x_prompt.shape
    bd, sd, _ = x_sample.shape
    n_p = bp * sp
    n_s = bd * sd
    n_pages = page_table.shape[1]
    past = n_pages * PAGE_SIZE
    dh = d // SB_HEADS
    n_pool = cache_k.shape[1]

    h = jnp.concatenate([x_prompt.reshape(n_p, d), x_sample.reshape(n_s, d)], axis=0)

    wqkv = sb_wqkv[0].astype(BF16)
    qk_gain = jnp.concatenate([jnp.tile(sb_q_gain[0], SB_HEADS),
                               jnp.tile(sb_k_gain[0], SB_HEADS)]).reshape(1, 2 * d).astype(F32)
    qk = _norm_matmul(h, wqkv[:, :2 * d], gain=norm_mix[0], epilogue=_headnorm_epilogue,
                      extras=(qk_gain,),
                      extra_specs=(lambda tm, tn: pl.BlockSpec((1, tn), lambda i, j: (0, j)),),
                      name="sb_qk_proj")
    v = _norm_matmul(h, wqkv[:, 2 * d:], gain=norm_mix[0], name="sb_v_proj")

    bias = sb_bias[0].astype(F32)
    bias_rep = jnp.broadcast_to(bias[:, None, None], (SB_HEADS, 1, LANES))
    att_p = _sb_prompt(qk, v, bias_rep, batch=bp, seq=sp, heads=SB_HEADS)
    bias_rows = jnp.broadcast_to(jnp.repeat(bias, sd)[:, None], (SB_HEADS * sd, LANES))
    att_s = _sb_sample(qk, v,
                       cache_k[0], cache_v[0],
                       page_table, bias_rows, row0=n_p, dec_batch=bd, t=sd, heads=SB_HEADS)
    att = jnp.concatenate([att_p, att_s], axis=0)
    h = _norm_matmul(att, sb_wo[0].astype(BF16), res=h, name="sb_wo")
    h = _peer_layer(h, norm_ffn[0], peer_wq[0], peer_keys1[0], peer_keys2[0], peer_u[0], peer_v[0])

    k_all = qk[:, d:]
    new_k_prompt = k_all[:n_p].reshape(1, bp, sp, SB_HEADS, dh)
    new_k_sample = k_all[n_p:].reshape(1, bd, sd, SB_HEADS, dh)
    new_v_prompt = v[:n_p].reshape(1, bp, sp, SB_HEADS, dh)
    new_v_sample = v[n_p:].reshape(1, bd, sd, SB_HEADS, dh)

    dk = d // RET_HEADS
    dv = 2 * d // RET_HEADS
    hq = RET_HEADS * dk
    hv = RET_HEADS * dv
    w = ret_wqkvg[0].astype(BF16)
    pos = jnp.concatenate([jnp.tile(jnp.arange(sp), bp), jnp.tile(past + jnp.arange(sd), bd)])
    cos, sin = _rope_tables(pos, dk)
    rope = functools.partial(_rope_epilogue, head_dim=dk, q_cols=hq, k_scale=dk ** -0.5)

    def half_spec(tm, tn):
        return pl.BlockSpec((tm, dk // 2), lambda i, j: (i, 0))

    rqk = _norm_matmul(h, w[:, :2 * hq], gain=norm_mix[1], epilogue=rope, extras=(cos, sin),
                       extra_specs=(half_spec, half_spec), name="ret_qk_proj")
    rvg = _norm_matmul(h, w[:, 2 * hq:], gain=norm_mix[1], name="ret_vg_proj")

    log_g = jnp.log1p(-jnp.exp2(-5.0 - jnp.arange(RET_HEADS, dtype=F32)))
    lg_rep = jnp.broadcast_to(log_g[:, None, None], (RET_HEADS, 1, LANES))
    ret_gain = ret_norm_gain[0].reshape(RET_HEADS, 1, dv).astype(F32)
    zero_state = jnp.zeros((bp, RET_HEADS, dk, dv), F32)
    o_p, r_p = _retention(rqk, rvg, zero_state, lg_rep, ret_gain,
                          row0=0, batch=bp, seq=sp, heads=RET_HEADS)
    o_s, r_s = _retention(rqk, rvg, state_ret[0], lg_rep, ret_gain,
                          row0=n_p, batch=bd, seq=sd, heads=RET_HEADS)
    o = jnp.concatenate([o_p, o_s], axis=0)
    h = _norm_matmul(o, ret_wo[0].astype(BF16), res=h, name="ret_wo")
    h = _peer_layer(h, norm_ffn[1], peer_wq[1], peer_keys1[1], peer_keys2[1], peer_u[1], peer_v[1])

    return (h[:n_p].reshape(bp, sp, d), h[n_p:].reshape(bd, sd, d),
            new_k_prompt, new_v_prompt, new_k_sample, new_v_sample,
            r_p[None], r_s[None])
```

```python
import functools

import jax
import jax.numpy as jnp
from jax import lax
from jax.experimental import pallas as pl
from jax.experimental.pallas import tpu as pltpu

F32 = jnp.float32
BF16 = jnp.bfloat16

SB_HEADS = 16
RET_HEADS = 8
PEER_HEADS = 8
PEER_KEY_DIM = 128
N_KEYS = 128
PEER_TOPK = 16
PAGE_SIZE = 128
Q_BLOCK = 128
RET_CHUNK = 128
ROPE_BASE = 10000.0
NORM_EPS = 1e-6

LANES = 128
SUBLANES = 8
BF16_ROWS = 16
VMEM_LIMIT = 52 * 1024 * 1024

PEER_TOKENS_PER_STEP = 8
PEER_PITCH_PAD = 4
PEER_GROUP_PAIRS = (256, 256, 256, 192, 64)
PEER_STARTS_BEFORE_WAIT = (0, 150, 150, 150, 276)
SB_PROMPT_Q_ROWS = 512
SB_HEAD_PITCH_PAD = 8
SB_PAGE_LOOKAHEAD = 2
MM_ROW_TILE_BYTES = 8 * 1024 * 1024


def _cparams(sem, vmem=VMEM_LIMIT):
    return pltpu.CompilerParams(dimension_semantics=sem, vmem_limit_bytes=vmem)


def _mm_kernel(*refs, has_gain, has_res, n_extra, epilogue, row_chunk):
    it = iter(refs)
    x_ref = next(it)
    g_ref = next(it) if has_gain else None
    w_ref = next(it)
    r_ref = next(it) if has_res else None
    extra_refs = [next(it) for _ in range(n_extra)]
    o_ref = next(it)
    xn_ref = next(it)
    tm = x_ref.shape[0]

    @pl.when(pl.program_id(1) == 0)
    def _():
        def chunk(c, carry):
            rows = pl.ds(pl.multiple_of(c * row_chunk, row_chunk), row_chunk)
            xv = x_ref[rows, :].astype(F32)
            if has_gain:
                ms = jnp.mean(xv * xv, axis=-1, keepdims=True)
                xv = (xv * lax.rsqrt(ms + NORM_EPS)) * g_ref[...]
            xn_ref[rows, :] = xv.astype(BF16)
            return carry
        lax.fori_loop(0, tm // row_chunk, chunk, 0)

    acc = jnp.dot(xn_ref[...], w_ref[...], preferred_element_type=F32)
    if has_res:
        acc = acc + r_ref[...]
    if epilogue is None:
        o_ref[...] = acc.astype(o_ref.dtype)
    else:
        epilogue(acc, o_ref, extra_refs, pl.program_id(1))


def _norm_matmul(x, w, *, gain=None, res=None, epilogue=None, extras=(), extra_specs=(),
                 tm=512, tn=512, name="mm"):
    n, k = x.shape
    m = w.shape[1]
    tn = min(tn, m)
    if n % (2 * tm) == 0 and 2 * tm * k * 4 <= MM_ROW_TILE_BYTES:
        tm = 2 * tm
    assert n % tm == 0 and m % tn == 0, (n, tm, m, tn)
    extra_specs = [make(tm, tn) for make in extra_specs]
    in_specs = [pl.BlockSpec((tm, k), lambda i, j: (i, 0))]
    args = [x]
    if gain is not None:
        in_specs.append(pl.BlockSpec((1, k), lambda i, j: (0, 0)))
        args.append(gain.reshape(1, k).astype(F32))
    in_specs.append(pl.BlockSpec((k, tn), lambda i, j: (0, j)))
    args.append(w)
    if res is not None:
        in_specs.append(pl.BlockSpec((tm, tn), lambda i, j: (i, j)))
        args.append(res)
    in_specs.extend(extra_specs)
    args.extend(extras)
    kern = functools.partial(_mm_kernel, has_gain=gain is not None, has_res=res is not None,
                             n_extra=len(extras), epilogue=epilogue, row_chunk=min(64, tm))
    return pl.pallas_call(
        kern,
        grid=(n // tm, m // tn),
        in_specs=in_specs,
        out_specs=pl.BlockSpec((tm, tn), lambda i, j: (i, j)),
        out_shape=jax.ShapeDtypeStruct((n, m), F32),
        scratch_shapes=[pltpu.VMEM((tm, k), BF16)],
        compiler_params=_cparams(("parallel", "arbitrary")),
        name=name,
    )(*args)


def _headnorm_epilogue(acc, o_ref, extra_refs, j):
    gain_ref, = extra_refs
    for g in range(acc.shape[1] // LANES):
        sl = slice(g * LANES, (g + 1) * LANES)
        y = acc[:, sl]
        ms = jnp.mean(y * y, axis=-1, keepdims=True)
        o_ref[:, sl] = (y * lax.rsqrt(ms + NORM_EPS)) * gain_ref[:, sl]


def _rope_epilogue(acc, o_ref, extra_refs, j, *, head_dim, q_cols, k_scale):
    cos_ref, sin_ref = extra_refs
    cos = cos_ref[...]
    sin = sin_ref[...]
    half = head_dim // 2
    tn = acc.shape[1]
    for g in range(tn // head_dim):
        scale = jnp.where(j * tn + g * head_dim >= q_cols, jnp.float32(k_scale), jnp.float32(1.0))
        x1 = acc[:, g * head_dim:g * head_dim + half]
        x2 = acc[:, g * head_dim + half:(g + 1) * head_dim]
        o_ref[:, g * head_dim:g * head_dim + half] = (x1 * cos - x2 * sin) * scale
        o_ref[:, g * head_dim + half:(g + 1) * head_dim] = (x1 * sin + x2 * cos) * scale


def _softplus(z):
    return jnp.maximum(z, 0.0) + jnp.log1p(jnp.exp(-jnp.abs(z)))


def _suffix_sum_weights(n):
    row = lax.broadcasted_iota(jnp.int32, (n, 2 * n), 0)
    col = lax.broadcasted_iota(jnp.int32, (n, 2 * n), 1)
    return jnp.where((col >= n) | (row > col), 1.0, 0.0).astype(BF16)


def _split_dot(x, w):
    hi = x.astype(BF16)
    lo = (x - hi.astype(F32)).astype(BF16)
    return jnp.dot(jnp.concatenate([hi, lo], axis=1), jnp.concatenate([w, w], axis=0),
                   preferred_element_type=F32)


def _sb_tile(z, valid, carry, lo_w):
    n = z.shape[1]
    sp = _softplus(z)
    log_beta = z - sp
    log_rest = -sp
    if valid is not None:
        log_rest = jnp.where(valid, log_rest, 0.0)
    cs = _split_dot(log_rest, lo_w)
    between = carry + cs[:, :n]
    a = jnp.exp(log_beta + between)
    if valid is not None:
        a = jnp.where(valid, a, 0.0)
    return a, carry + cs[:, n:]


def _sb_prompt_kernel(bias_ref, q_ref, k_ref, v_ref, o_ref, *, scale):
    i = pl.program_id(2)
    tq = q_ref.shape[0]
    tk = LANES
    sub = tq // tk
    q = q_ref[...].astype(BF16)
    bias = bias_ref[...]
    lo_w = _suffix_sum_weights(tk)
    row = lax.broadcasted_iota(jnp.int32, (tq, tk), 0)
    col = lax.broadcasted_iota(jnp.int32, (tq, tk), 1)

    grp = next(n for n in (4, 2, 1) if sub % n == 0)

    def tiles(kg, acc, carry, valids):
        rows = pl.ds(pl.multiple_of(kg * (grp * tk), grp * tk), grp * tk)
        k = k_ref[rows, :].astype(BF16)
        v = v_ref[rows, :].astype(BF16)
        z = lax.dot_general(q, k, (((1,), (1,)), ((), ())), preferred_element_type=F32)
        parts = [None] * grp
        for u in reversed(range(grp)):
            zu = z[:, u * tk:(u + 1) * tk] * scale + bias
            parts[u], carry = _sb_tile(zu, valids[u], carry, lo_w)
        a = parts[0] if grp == 1 else jnp.concatenate(parts, axis=1)
        acc = acc + jnp.dot(a.astype(BF16), v, preferred_element_type=F32)
        return acc, carry

    acc = jnp.zeros((tq, v_ref.shape[1]), F32)
    carry = jnp.zeros((tq, tk), F32)
    for dg in reversed(range(sub // grp)):
        valids = [col + (dg * grp + u) * tk < row for u in range(grp)]
        acc, carry = tiles(i * (sub // grp) + dg, acc, carry, valids)

    def body(jj, c):
        return tiles(i * (sub // grp) - 1 - jj, c[0], c[1], [None] * grp)

    acc, carry = lax.fori_loop(0, i * (sub // grp), body, (acc, carry))
    o_ref[...] = acc.astype(o_ref.dtype)


def _sb_prompt(qk, v, bias_rep, *, batch, seq, heads):
    dh = LANES
    d = heads * dh
    tq = SB_PROMPT_Q_ROWS if seq % SB_PROMPT_Q_ROWS == 0 else LANES
    nq = seq // tq
    kern = functools.partial(_sb_prompt_kernel, scale=dh ** -0.5)
    return pl.pallas_call(
        kern,
        grid=(batch, heads, nq),
        in_specs=[
            pl.BlockSpec((None, 1, LANES), lambda b, h, i: (h, 0, 0)),
            pl.BlockSpec((tq, dh), lambda b, h, i: (b * nq + i, h)),
            pl.BlockSpec((seq, dh), lambda b, h, i: (b, heads + h)),
            pl.BlockSpec((seq, dh), lambda b, h, i: (b, h)),
        ],
        out_specs=pl.BlockSpec((tq, dh), lambda b, h, i: (b * nq + i, h)),
        out_shape=jax.ShapeDtypeStruct((batch * seq, d), F32),
        compiler_params=_cparams(("parallel", "parallel", "arbitrary")),
        name="sb_prompt",
    )(bias_rep, qk, qk, v)


def _sb_sample_kernel(pt_ref, bias_ref, q_ref, kn_ref, vn_ref, ck_ref, cv_ref, o_ref,
                      acc_ref, carry_ref, kbuf_ref, vbuf_ref, sem_ref, *, heads, n_pages, scale):
    b = pl.program_id(0)
    s = pl.program_id(1)
    t = q_ref.shape[0]
    pitch = heads + SB_HEAD_PITCH_PAD

    n_slots = SB_PAGE_LOOKAHEAD + 1
    total_pages = pl.num_programs(0) * n_pages

    def page_copy(cache_ref, buf_ref, page, slot, which):
        return pltpu.make_async_copy(
            cache_ref.at[page],
            buf_ref.at[pl.ds(slot * PAGE_SIZE, PAGE_SIZE), pl.ds(0, heads), :],
            sem_ref.at[which, slot])

    def start_page(k):
        @pl.when(k < total_pages)
        def _():
            page = pt_ref[k // n_pages, n_pages - 1 - k % n_pages]
            slot = k % n_slots
            page_copy(ck_ref, kbuf_ref, page, slot, 0).start()
            page_copy(cv_ref, vbuf_ref, page, slot, 1).start()
    dh = LANES
    hq = heads * t
    bias = bias_ref[...]
    lo_w = _suffix_sum_weights(LANES)

    def q_head(h):
        qh = q_ref[:, h * dh:(h + 1) * dh]
        pad = jnp.zeros((BF16_ROWS - t, dh), F32)
        return jnp.concatenate([qh, pad], axis=0).astype(BF16)

    def attend(k_of, v_of, valid):
        zs = []
        for h in range(heads):
            zh = lax.dot_general(q_head(h), k_of(h), (((1,), (1,)), ((), ())),
                                 preferred_element_type=F32)
            zs.append(zh[:t])
        z = jnp.concatenate(zs, axis=0) * scale + bias
        a, carry = _sb_tile(z, valid, carry_ref[...], lo_w)
        carry_ref[...] = carry
        for h in range(heads):
            ah = a[h * t:(h + 1) * t]
            ah = jnp.concatenate([ah, jnp.zeros((BF16_ROWS - t, LANES), F32)], axis=0)
            oh = jnp.dot(ah.astype(BF16), v_of(h), preferred_element_type=F32)
            acc_ref[:, h * dh:(h + 1) * dh] += oh[:t]

    @pl.when((s == 0) & (b == 0))
    def _():
        for k0 in range(SB_PAGE_LOOKAHEAD):
            start_page(jnp.int32(k0))

    @pl.when(s == 0)
    def _():
        acc_ref[...] = jnp.zeros_like(acc_ref)
        carry_ref[...] = jnp.zeros_like(carry_ref)
        row = lax.broadcasted_iota(jnp.int32, (hq, LANES), 0)
        col = lax.broadcasted_iota(jnp.int32, (hq, LANES), 1)
        valid = col < lax.rem(row, t)
        zpad = jnp.zeros((LANES - t, dh), F32)

        def k_of(h):
            return jnp.concatenate([kn_ref[:, h * dh:(h + 1) * dh], zpad], axis=0).astype(BF16)

        def v_of(h):
            return jnp.concatenate([vn_ref[:, h * dh:(h + 1) * dh], zpad], axis=0).astype(BF16)

        attend(k_of, v_of, valid)

    @pl.when(s > 0)
    def _():
        k = b * n_pages + s - 1
        slot = k % n_slots
        start_page(k + SB_PAGE_LOOKAHEAD)
        page_copy(ck_ref, kbuf_ref, 0, slot, 0).wait()
        page_copy(cv_ref, vbuf_ref, 0, slot, 1).wait()
        flat = (n_slots * PAGE_SIZE * pitch, LANES)

        def k_of(h):
            return kbuf_ref.reshape(*flat)[pl.ds(slot * PAGE_SIZE * pitch + h, PAGE_SIZE,
                                                 stride=pitch), :].astype(BF16)

        def v_of(h):
            return vbuf_ref.reshape(*flat)[pl.ds(slot * PAGE_SIZE * pitch + h, PAGE_SIZE,
                                                 stride=pitch), :].astype(BF16)

        attend(k_of, v_of, None)

    @pl.when(s == pl.num_programs(1) - 1)
    def _():
        o_ref[...] = acc_ref[...]


def _sb_sample(qk, v, cache_k, cache_v, page_table, bias_rows, *, row0, dec_batch, t, heads):
    dh = LANES
    d = heads * dh
    n_pages = page_table.shape[1]
    rb0 = row0 // t
    n_slots = SB_PAGE_LOOKAHEAD + 1
    ring = pltpu.VMEM((n_slots * PAGE_SIZE, heads + SB_HEAD_PITCH_PAD, dh), F32)

    kern = functools.partial(_sb_sample_kernel, heads=heads, n_pages=n_pages, scale=dh ** -0.5)
    grid_spec = pltpu.PrefetchScalarGridSpec(
        num_scalar_prefetch=1,
        grid=(dec_batch, n_pages + 1),
        in_specs=[
            pl.BlockSpec((heads * t, LANES), lambda b, s, pt: (0, 0)),
            pl.BlockSpec((t, d), lambda b, s, pt: (rb0 + b, 0)),
            pl.BlockSpec((t, d), lambda b, s, pt: (rb0 + b, 1)),
            pl.BlockSpec((t, d), lambda b, s, pt: (rb0 + b, 0)),
            pl.BlockSpec(memory_space=pl.ANY),
            pl.BlockSpec(memory_space=pl.ANY),
        ],
        out_specs=pl.BlockSpec((t, d), lambda b, s, pt: (b, 0)),
        scratch_shapes=[pltpu.VMEM((t, d), F32), pltpu.VMEM((heads * t, LANES), F32),
                        ring, ring, pltpu.SemaphoreType.DMA((2, n_slots))],
    )
    return pl.pallas_call(
        kern,
        grid_spec=grid_spec,
        out_shape=jax.ShapeDtypeStruct((dec_batch * t, d), F32),
        compiler_params=_cparams(("arbitrary", "arbitrary")),
        name="sb_sample",
    )(page_table, bias_rows, qk, qk, v, cache_k, cache_v)


def _retention_kernel(lg_ref, gain_ref, q_ref, k_ref, v_ref, g_ref, s_ref, o_ref, so_ref, r_ref,
                      *, chunk_len):
    c = pl.program_id(2)
    rows_in = q_ref.shape[0]
    cp = LANES
    assert rows_in <= cp
    hp = s_ref.shape[0]
    dk = q_ref.shape[1] // hp
    dv = v_ref.shape[1] // hp

    @pl.when(c == 0)
    def _():
        r_ref[...] = s_ref[...]

    def load(ref, lo, width):
        x = ref[:, lo:lo + width]
        if rows_in < cp:
            x = jnp.concatenate([x, jnp.zeros((cp - rows_in, width), F32)], axis=0)
        return x

    row = lax.broadcasted_iota(jnp.int32, (cp, LANES), 0).astype(F32)
    col = lax.broadcasted_iota(jnp.int32, (cp, LANES), 1).astype(F32)
    diff = row - col
    for u in range(hp):
        lg = lg_ref[u]
        decay_query = jnp.exp((row + 1.0) * lg)
        decay_key = jnp.exp((chunk_len - 1.0 - row) * lg)
        decay_chunk = jnp.exp(chunk_len * lg)
        decay_intra = jnp.where(diff >= 0, jnp.exp(jnp.maximum(diff, 0.0) * lg), 0.0)

        q = load(q_ref, u * dk, dk)
        k = load(k_ref, u * dk, dk)
        v = load(v_ref, u * dv, dv).astype(BF16)
        qb = q.astype(BF16)
        kb = k.astype(BF16)
        att = lax.dot_general(qb, kb, (((1,), (1,)), ((), ())), preferred_element_type=F32)
        att = att * decay_intra
        inner = jnp.dot(att.astype(BF16), v, preferred_element_type=F32)
        r = r_ref[u]
        cross = jnp.dot(qb, r.astype(BF16), preferred_element_type=F32)
        kd = jnp.concatenate([k[:, s * LANES:(s + 1) * LANES] * decay_key
                              for s in range(dk // LANES)], axis=1).astype(BF16)
        upd = lax.dot_general(kd, v, (((0,), (0,)), ((), ())), preferred_element_type=F32)
        for s in range(dv // LANES):
            sl = slice(s * LANES, (s + 1) * LANES)
            r_ref[u, :, sl] = decay_chunk * r[:, sl] + upd[:, sl]
        out = jnp.concatenate([inner[:, s * LANES:(s + 1) * LANES]
                               + cross[:, s * LANES:(s + 1) * LANES] * decay_query
                               for s in range(dv // LANES)], axis=1)[:rows_in]
        ms = jnp.mean(out * out, axis=-1, keepdims=True)
        o = (out * lax.rsqrt(ms + NORM_EPS)) * gain_ref[u]
        g = g_ref[:, u * dv:(u + 1) * dv]
        o_ref[:, u * dv:(u + 1) * dv] = (g * jax.nn.sigmoid(g)) * o

    @pl.when(c == pl.num_programs(2) - 1)
    def _():
        so_ref[...] = r_ref[...]


def _retention(qk, vg, state, lg_rep, gain, *, row0, batch, seq, heads):
    dk = qk.shape[1] // (2 * heads)
    dv = vg.shape[1] // (2 * heads)
    assert dk % LANES == 0 and dv % LANES == 0
    c = RET_CHUNK if seq % RET_CHUNK == 0 else seq
    nc = seq // c
    rb0 = row0 // c
    hp = 2 if heads % 2 == 0 else 1
    hb = heads // hp

    def rows(b, h, ci):
        return rb0 + b * nc + ci

    kern = functools.partial(_retention_kernel, chunk_len=float(c))
    return pl.pallas_call(
        kern,
        grid=(batch, hb, nc),
        in_specs=[
            pl.BlockSpec((hp, 1, LANES), lambda b, h, ci: (h, 0, 0)),
            pl.BlockSpec((hp, 1, dv), lambda b, h, ci: (h, 0, 0)),
            pl.BlockSpec((c, hp * dk), lambda b, h, ci: (rows(b, h, ci), h)),
            pl.BlockSpec((c, hp * dk), lambda b, h, ci: (rows(b, h, ci), hb + h)),
            pl.BlockSpec((c, hp * dv), lambda b, h, ci: (rows(b, h, ci), h)),
            pl.BlockSpec((c, hp * dv), lambda b, h, ci: (rows(b, h, ci), hb + h)),
            pl.BlockSpec((None, hp, dk, dv), lambda b, h, ci: (b, h, 0, 0)),
        ],
        out_specs=[
            pl.BlockSpec((c, hp * dv), lambda b, h, ci: (b * nc + ci, h)),
            pl.BlockSpec((None, hp, dk, dv), lambda b, h, ci: (b, h, 0, 0)),
        ],
        out_shape=[
            jax.ShapeDtypeStruct((batch * seq, heads * dv), F32),
            jax.ShapeDtypeStruct((batch, heads, dk, dv), F32),
        ],
        scratch_shapes=[pltpu.VMEM((hp, dk, dv), F32)],
        compiler_params=_cparams(("parallel", "parallel", "arbitrary")),
        name="retention",
    )(lg_rep, gain, qk, qk, vg, vg, state)


def _topk_rows(problems, k):
    t_ = problems[0][0].shape[1]
    krow = lax.broadcasted_iota(jnp.int32, (k, t_), 0)
    rows = [lax.broadcasted_iota(jnp.int32, s.shape, 0) for s, _ in problems]

    def body(r, carry):
        out = []
        for (s, vals, idxs), (_, payload), rw in zip(carry, problems, rows):
            m = jnp.max(s, axis=0, keepdims=True)
            am = jnp.min(jnp.where(s == m, rw, s.shape[0]), axis=0, keepdims=True)
            hit = rw == am
            if payload is None:
                pv = am
            else:
                pv = jnp.max(jnp.where(hit, payload, -1), axis=0, keepdims=True)
            vals = jnp.where(krow == r, m, vals)
            idxs = jnp.where(krow == r, pv, idxs)
            out.append((jnp.where(hit, -jnp.inf, s), vals, idxs))
        return tuple(out)

    init = tuple((s, jnp.zeros((k, t_), F32), jnp.zeros((k, t_), jnp.int32)) for s, _ in problems)
    res = lax.fori_loop(0, k, body, init)
    return [(vals, idxs) for _, vals, idxs in res]


def _peer_candidates(v1, i1, v2, i2):
    k = v1.shape[0]
    t_ = v1.shape[1]
    brow = lax.broadcasted_iota(jnp.int32, (SUBLANES, t_), 0)
    sums = [v1[0:1] + v2]
    ids = [i1[0:1] * N_KEYS + i2]
    tail = k // 2
    for a in range(1, tail):
        keep = brow < (k // (a + 1))
        sums.append(jnp.where(keep, v1[a:a + 1] + v2[0:SUBLANES], -jnp.inf))
        ids.append(i1[a:a + 1] * N_KEYS + i2[0:SUBLANES])
    sums.append(v1[tail:k] + v2[0:1])
    ids.append(i1[tail:k] * N_KEYS + i2[0:1])
    return jnp.concatenate(sums, axis=0), jnp.concatenate(ids, axis=0)


def _peer_route_kernel(x_ref, g_ref, wq_ref, keys_ref, xn_ref, idx_ref, gate_ref):
    x = x_ref[...]
    ms = jnp.mean(x * x, axis=-1, keepdims=True)
    xn = (x * lax.rsqrt(ms + NORM_EPS)) * g_ref[...]
    xn_ref[...] = xn
    q = jnp.dot(xn.astype(BF16), wq_ref[...], preferred_element_type=F32)
    st = lax.dot_general(keys_ref[...], q.astype(BF16), (((1,), (1,)), ((), ())),
                         preferred_element_type=F32)
    kk = PEER_TOPK
    cands = []
    pair_heads = 1
    for h0 in range(0, PEER_HEADS, pair_heads):
        slabs = [(st[r * N_KEYS:(r + 1) * N_KEYS], None)
                 for r in range(2 * h0, 2 * (h0 + pair_heads))]
        tops = _topk_rows(slabs, kk)
        for u in range(pair_heads):
            (v1, i1), (v2, i2) = tops[2 * u], tops[2 * u + 1]
            cands.append(_peer_candidates(v1, i1, v2, i2))
    group = 4
    for h0 in range(0, PEER_HEADS, group):
        picked = _topk_rows(cands[h0:h0 + group], kk)
        for h, (sc, idx) in zip(range(h0, h0 + group), picked):
            e = jnp.exp(sc - sc[0:1])
            gate = e / jnp.sum(e, axis=0, keepdims=True)
            idx_ref[h * kk:(h + 1) * kk, :] = idx
            gate_ref[h * kk:(h + 1) * kk, :] = gate


def _peer_route(h, gain, wq, keys_bd, *, tb=128):
    n, d = h.shape
    hk = PEER_HEADS * PEER_TOPK
    return pl.pallas_call(
        _peer_route_kernel,
        grid=(n // tb,),
        in_specs=[
            pl.BlockSpec((tb, d), lambda i: (i, 0)),
            pl.BlockSpec((1, d), lambda i: (0, 0)),
            pl.BlockSpec(wq.shape, lambda i: (0, 0)),
            pl.BlockSpec(keys_bd.shape, lambda i: (0, 0)),
        ],
        out_specs=[
            pl.BlockSpec((tb, d), lambda i: (i, 0)),
            pl.BlockSpec((hk, tb), lambda i: (0, i)),
            pl.BlockSpec((hk, tb), lambda i: (0, i)),
        ],
        out_shape=[
            jax.ShapeDtypeStruct((n, d), F32),
            jax.ShapeDtypeStruct((hk, n), jnp.int32),
            jax.ShapeDtypeStruct((hk, n), F32),
        ],
        compiler_params=_cparams(("parallel",)),
        name="peer_route",
    )(h, gain.reshape(1, d).astype(F32), wq, keys_bd)


def _gelu_tanh(x):
    c = 0.7978845608028654
    return 0.5 * x * (1.0 + jnp.tanh(c * (x + 0.044715 * (x * x * x))))


def _peer_expert_kernel(idx_ref, gate_ref, xn_ref, h_ref, tab_ref, o_ref, buf0_ref, buf1_ref,
                        sem_ref, *, d, n_blocks):
    g = pl.program_id(0)
    tb, hk = idx_ref.shape
    pairs = tb * hk
    half = d // 2
    wr = half // LANES
    rows_per = 2 * wr
    pitch = rows_per + PEER_PITCH_PAD
    cw = 2 if wr % 2 == 0 else 1
    bufs = (buf0_ref, buf1_ref)

    assert sum(PEER_GROUP_PAIRS) == pairs and len(PEER_STARTS_BEFORE_WAIT) == len(PEER_GROUP_PAIRS)
    n_grp = len(PEER_GROUP_PAIRS)
    g_lo = [sum(PEER_GROUP_PAIRS[:k]) for k in range(n_grp)]
    group_of = [k for k, n in enumerate(PEER_GROUP_PAIRS) for _ in range(n)]
    kc = cw * LANES
    n_chunks = wr // cw

    def issue(slot, p0, p1):
        for p in range(p0, p1):
            pltpu.make_async_copy(tab_ref.at[idx_ref[p // hk, p % hk]],
                                  bufs[slot].at[pl.ds(p * pitch, rows_per), :],
                                  sem_ref.at[slot, group_of[p]]).start(priority=p % 2)

    def wait(slot, grp):
        part = bufs[slot].at[pl.ds(0, PEER_GROUP_PAIRS[grp] * rows_per), :]
        pltpu.make_async_copy(part, part, sem_ref.at[slot, grp]).wait()

    def words(slot, r0, p0, n):
        parts = [pltpu.bitcast(bufs[slot][pl.ds(p0 * pitch + r0 + r, n, stride=pitch), :], BF16)
                 for r in range(cw)]
        return parts[0] if cw == 1 else jnp.concatenate(parts, axis=1)

    def split_x():
        x = xn_ref[...]
        return jnp.concatenate([x[:, :half], x[:, half:]], axis=0).astype(BF16)

    def first_stage(slot, grp, x2):
        gp = PEER_GROUP_PAIRS[grp]
        r = jnp.zeros((2 * tb, 2 * gp), F32)
        for c in range(n_chunks):
            r = r + lax.dot_general(x2[:, c * kc:(c + 1) * kc], words(slot, c * cw, g_lo[grp], gp),
                                    (((1,), (1,)), ((), ())), preferred_element_type=F32)
        return r

    def coefficients(rs, p0):
        r = rs[0] if len(rs) == 1 else jnp.concatenate(rs, axis=1)
        n2 = r.shape[1]
        act = r[:tb] + pltpu.roll(r[tb:], n2 - 1, axis=1)
        gate = jnp.concatenate([gate_ref[...]] * tb, axis=1)[:, 2 * p0:2 * p0 + n2]
        row = lax.broadcasted_iota(jnp.int32, (tb, n2), 0)
        lane = lax.broadcasted_iota(jnp.int32, (tb, n2), 1) + 2 * p0
        own = ((lane >= row * (2 * hk)) & (lane < (row + 1) * (2 * hk))
               & (jnp.bitwise_and(lane, 1) == 0))
        coef_lo = jnp.where(own, gate * _gelu_tanh(act), 0.0)
        coef_hi = pltpu.roll(coef_lo, 1, axis=1)
        return jnp.concatenate([coef_lo, coef_hi], axis=0).astype(BF16)

    def second_stage(slot, parts, before_chunk=lambda c: None):
        for c in range(n_chunks):
            before_chunk(c)
            out = jnp.zeros((2 * tb, kc), F32)
            for coef, p0, n in parts:
                out = out + jnp.dot(coef, words(slot, wr + c * cw, p0, n),
                                    preferred_element_type=F32)
            lo = slice(c * kc, (c + 1) * kc)
            hi = slice(half + c * kc, half + (c + 1) * kc)
            o_ref[:, lo] = h_ref[:, lo] + out[:tb]
            o_ref[:, hi] = h_ref[:, hi] + out[tb:]

    @pl.when(g == 0)
    def _():
        issue(0, 0, pairs)

    cuts = [min(pairs, sum(PEER_STARTS_BEFORE_WAIT[:k])) for k in range(n_grp + 1)] + [pairs]
    p_last = g_lo[n_grp - 1]

    def block(slot, issue_part, before_chunk):
        x2 = split_x()
        rs = []
        for grp in range(n_grp - 1):
            issue_part(grp)
            wait(slot, grp)
            rs.append(first_stage(slot, grp, x2))
        parts = [(coefficients(rs, 0), 0, p_last)] if rs else []
        issue_part(n_grp - 1)
        wait(slot, n_grp - 1)
        tail = coefficients([first_stage(slot, n_grp - 1, x2)], p_last)
        second_stage(slot, parts + [(tail, p_last, pairs - p_last)], before_chunk)

    for par in range(2):
        @pl.when((g >= 1) & (g < n_blocks) & (g % 2 == par))
        def _():
            late0 = cuts[n_grp]
            per = -(-(pairs - late0) // n_chunks)
            block(1 - par,
                  lambda grp: issue(par, cuts[grp], cuts[grp + 1]),
                  lambda c: issue(par, min(pairs, late0 + c * per),
                                  min(pairs, late0 + (c + 1) * per)))

    @pl.when(g == n_blocks)
    def _():
        block((n_blocks - 1) % 2, lambda grp: None, lambda c: None)


def _peer_experts(idx, gate2, xn, h, table):
    n, d = h.shape
    hk = idx.shape[1]
    tb = PEER_TOKENS_PER_STEP
    n_blocks = n // tb
    rows_per = table.shape[1]
    assert rows_per == d // LANES and n % tb == 0
    kern = functools.partial(_peer_expert_kernel, d=d, n_blocks=n_blocks)

    def cur(g):
        return (jnp.maximum(g - 1, 0), 0)

    slot = pltpu.VMEM((tb * hk * (rows_per + PEER_PITCH_PAD), LANES), jnp.uint32)
    return pl.pallas_call(
        kern,
        grid=(n_blocks + 1,),
        in_specs=[
            pl.BlockSpec((tb, hk), lambda g: (jnp.minimum(g, n_blocks - 1), 0),
                         memory_space=pltpu.SMEM),
            pl.BlockSpec((tb, 2 * hk), cur),
            pl.BlockSpec((tb, d), cur),
            pl.BlockSpec((tb, d), cur),
            pl.BlockSpec(memory_space=pl.ANY),
        ],
        out_specs=pl.BlockSpec((tb, d), cur),
        out_shape=jax.ShapeDtypeStruct((n, d), F32),
        scratch_shapes=[slot, slot, pltpu.SemaphoreType.DMA((2, len(PEER_GROUP_PAIRS)))],
        compiler_params=_cparams(("arbitrary",)),
        name="peer_experts",
    )(idx, gate2, xn, h, table)


def _pack_bf16_halves(w):
    e, d = w.shape
    bits = lax.bitcast_convert_type(w.astype(BF16), jnp.uint16).astype(jnp.uint32)
    words = bits[:, :d // 2] | (bits[:, d // 2:] << 16)
    return words.reshape(e, d // 2 // LANES, LANES)


def _peer_keys_blockdiag(keys1, keys2):
    half = PEER_KEY_DIM // 2
    eye = jnp.eye(PEER_HEADS * 2, dtype=F32)
    keys = jnp.stack([keys1, keys2]).astype(F32)
    keys = jnp.tile(keys, (PEER_HEADS, 1, 1))
    bd = jnp.einsum("gnc,gf->gnfc", keys, eye)
    return bd.reshape(PEER_HEADS * 2 * N_KEYS, PEER_HEADS * 2 * half).astype(BF16)


def _peer_layer(h, gain, wq, keys1, keys2, u, v):
    xn, idx_t, gate_t = _peer_route(h, gain, wq.astype(BF16), _peer_keys_blockdiag(keys1, keys2))
    table = jnp.concatenate([_pack_bf16_halves(u), _pack_bf16_halves(v)], axis=1)
    gate2 = jnp.repeat(gate_t.T, 2, axis=1)
    return _peer_experts(idx_t.T, gate2, xn, h, table)


def _rope_tables(positions, head_dim):
    half = head_dim // 2
    inv = ROPE_BASE ** (-jnp.arange(half, dtype=F32) / half)
    ang = positions.astype(F32)[:, None] * inv[None, :]
    return jnp.cos(ang), jnp.sin(ang)


def kernel(x_prompt, x_sample, cache_k, cache_v, state_ret, page_table, norm_mix, norm_ffn,
           sb_wqkv, sb_q_gain, sb_k_gain, sb_bias, sb_wo, ret_wqkvg, ret_norm_gain, ret_wo,
           peer_wq, peer_keys1, peer_keys2, peer_u, peer_v):
    bp, sp, d = x_prompt.shape
    bd, sd, _ = x_sample.shape
    n_p = bp * sp
    n_s = bd * sd
    n_pages = page_table.shape[1]
    past = n_pages * PAGE_SIZE
    dh = d // SB_HEADS
    n_pool = cache_k.shape[1]

    h = jnp.concatenate([x_prompt.reshape(n_p, d), x_sample.reshape(n_s, d)], axis=0)

    wqkv = sb_wqkv[0].astype(BF16)
    qk_gain = jnp.concatenate([jnp.tile(sb_q_gain[0], SB_HEADS),
                               jnp.tile(sb_k_gain[0], SB_HEADS)]).reshape(1, 2 * d).astype(F32)
    qk = _norm_matmul(h, wqkv[:, :2 * d], gain=norm_mix[0], epilogue=_headnorm_epilogue,
                      extras=(qk_gain,),
                      extra_specs=(lambda tm, tn: pl.BlockSpec((1, tn), lambda i, j: (0, j)),),
                      name="sb_qk_proj")
    v = _norm_matmul(h, wqkv[:, 2 * d:], gain=norm_mix[0], name="sb_v_proj")

    bias = sb_bias[0].astype(F32)
    bias_rep = jnp.broadcast_to(bias[:, None, None], (SB_HEADS, 1, LANES))
    att_p = _sb_prompt(qk, v, bias_rep, batch=bp, seq=sp, heads=SB_HEADS)
    bias_rows = jnp.broadcast_to(jnp.repeat(bias, sd)[:, None], (SB_HEADS * sd, LANES))
    att_s = _sb_sample(qk, v,
                       cache_k[0], cache_v[0],
                       page_table, bias_rows, row0=n_p, dec_batch=bd, t=sd, heads=SB_HEADS)
    att = jnp.concatenate([att_p, att_s], axis=0)
    h = _norm_matmul(att, sb_wo[0].astype(BF16), res=h, name="sb_wo")
    h = _peer_layer(h, norm_ffn[0], peer_wq[0], peer_keys1[0], peer_keys2[0], peer_u[0], peer_v[0])

    k_all = qk[:, d:]
    new_k_prompt = k_all[:n_p].reshape(1, bp, sp, SB_HEADS, dh)
    new_k_sample = k_all[n_p:].reshape(1, bd, sd, SB_HEADS, dh)
    new_v_prompt = v[:n_p].reshape(1, bp, sp, SB_HEADS, dh)
    new_v_sample = v[n_p:].reshape(1, bd, sd, SB_HEADS, dh)

    dk = d // RET_HEADS
    dv = 2 * d // RET_HEADS
    hq = RET_HEADS * dk
    hv = RET_HEADS * dv
    w = ret_wqkvg[0].astype(BF16)
    pos = jnp.concatenate([jnp.tile(jnp.arange(sp), bp), jnp.tile(past + jnp.arange(sd), bd)])
    cos, sin = _rope_tables(pos, dk)
    rope = functools.partial(_rope_epilogue, head_dim=dk, q_cols=hq, k_scale=dk ** -0.5)

    def half_spec(tm, tn):
        return pl.BlockSpec((tm, dk // 2), lambda i, j: (i, 0))

    rqk = _norm_matmul(h, w[:, :2 * hq], gain=norm_mix[1], epilogue=rope, extras=(cos, sin),
                       extra_specs=(half_spec, half_spec), name="ret_qk_proj")
    rvg = _norm_matmul(h, w[:, 2 * hq:], gain=norm_mix[1], name="ret_vg_proj")

    log_g = jnp.log1p(-jnp.exp2(-5.0 - jnp.arange(RET_HEADS, dtype=F32)))
    lg_rep = jnp.broadcast_to(log_g[:, None, None], (RET_HEADS, 1, LANES))
    ret_gain = ret_norm_gain[0].reshape(RET_HEADS, 1, dv).astype(F32)
    zero_state = jnp.zeros((bp, RET_HEADS, dk, dv), F32)
    o_p, r_p = _retention(rqk, rvg, zero_state, lg_rep, ret_gain,
                          row0=0, batch=bp, seq=sp, heads=RET_HEADS)
    o_s, r_s = _retention(rqk, rvg, state_ret[0], lg_rep, ret_gain,
                          row0=n_p, batch=bd, seq=sd, heads=RET_HEADS)
    o = jnp.concatenate([o_p, o_s], axis=0)
    h = _norm_matmul(o, ret_wo[0].astype(BF16), res=h, name="ret_wo")
    h = _peer_layer(h, norm_ffn[1], peer_wq[1], peer_keys1[1], peer_keys2[1], peer_u[1], peer_v[1])

    return (h[:n_p].reshape(bp, sp, d), h[n_p:].reshape(bd, sd, d),
            new_k_prompt, new_v_prompt, new_k_sample, new_v_sample,
            r_p[None], r_s[None])
```
